```python
import math, functools
import jax, jax.numpy as jnp
from jax import lax
import numpy as np

D_MODEL = 1024
BATCH = 2
SEQ = 8192
DEPTH = 4
DEC_BATCH = 128
DEC_SEQ = 1
PAST_LEN = 2048
PAGE_SIZE = 128

N_HEADS = 4
HEAD_DIM = 64
QK_DIM = 2 * HEAD_DIM
V_DIM = 2 * HEAD_DIM
ATTN_WIDTH = N_HEADS * V_DIM
POOL_WINDOWS = (2, 4, 8, 16)
POOL_GROUPS = len(POOL_WINDOWS)
POOL_WIDTH = D_MODEL // 2
POOL_GROUP_DIM = POOL_WIDTH // POOL_GROUPS
POOL_STATE = max(POOL_WINDOWS) - 1
D_FF = 2816
N_BUCKETS = 32
MAX_DISTANCE = 128
Q_BLOCK = 128
NORM_EPS = 1e-6
NEG_INF = -1e30
Q_W = N_HEADS * QK_DIM
K_W = N_HEADS * QK_DIM
V_W = N_HEADS * V_DIM
IN_WIDTH = Q_W + K_W + V_W + POOL_WIDTH + 2 * D_MODEL

kernel_name = "hybrid_diffattn_pool_gated_decoder_step"


def rms_norm(x, g):
    xf = x.astype(jnp.float32)
    y = xf * lax.rsqrt(jnp.mean(xf * xf, axis=-1, keepdims=True) + NORM_EPS)
    return (y * g.astype(jnp.float32)).astype(x.dtype)


def half_ffn(x, pre, post, wg, wu, wd):
    u = rms_norm(x, pre)
    f = (jax.nn.silu(u @ wg) * (u @ wu)) @ wd
    return x + 0.5 * rms_norm(f, post)


def t5_bucket(qpos, kpos):
    n = jnp.maximum(qpos[:, None] - kpos[None, :], 0)
    max_exact = N_BUCKETS // 2
    nf = jnp.maximum(n, 1).astype(jnp.float32)
    large = max_exact + (jnp.log(nf / max_exact) / math.log(MAX_DISTANCE / max_exact)
                         * (N_BUCKETS - max_exact)).astype(jnp.int32)
    large = jnp.minimum(large, N_BUCKETS - 1)
    return jnp.where(n < max_exact, n, large)


def diff_attn_core(q, k, v, qpos, kpos, lam, rel_bias):
    q1, q2 = q[..., :HEAD_DIM], q[..., HEAD_DIM:]
    k1, k2 = k[..., :HEAD_DIM], k[..., HEAD_DIM:]
    bias = jnp.transpose(rel_bias[t5_bucket(qpos, kpos)], (2, 0, 1)).astype(jnp.float32)
    mask = kpos[None, :] <= qpos[:, None]
    scale = HEAD_DIM ** -0.5

    def probs(qa, ka):
        s = jnp.einsum('bqhd,bkhd->bhqk', qa, ka).astype(jnp.float32) * scale + bias
        return jax.nn.softmax(jnp.where(mask, s, NEG_INF), axis=-1)

    w = probs(q1, k1) - lam * probs(q2, k2)
    return jnp.einsum('bhqk,bkhv->bqhv', w, v.astype(jnp.float32))


def attend_prompt(q, k, v, lam, rel_bias):
    b, t = q.shape[0], q.shape[1]
    nb = t // Q_BLOCK
    qb = q.reshape(b, nb, Q_BLOCK, N_HEADS, QK_DIM).transpose(1, 0, 2, 3, 4)
    kpos = jnp.arange(t)

    def block(args):
        qi, i = args
        qpos = i * Q_BLOCK + jnp.arange(Q_BLOCK)
        return diff_attn_core(qi, k, v, qpos, kpos, lam, rel_bias)

    o = lax.map(block, (qb, jnp.arange(nb)))
    return o.transpose(1, 0, 2, 3, 4).reshape(b, t, N_HEADS, V_DIM)


def attend_sample(q, k, v, lam, rel_bias, k_past, v_past):
    t, p = q.shape[1], k_past.shape[1]
    k_all = jnp.concatenate([k_past, k], axis=1)
    v_all = jnp.concatenate([v_past, v], axis=1)
    qpos = p + jnp.arange(t)
    kpos = jnp.arange(p + t)
    return diff_attn_core(q, k_all, v_all, qpos, kpos, lam, rel_bias)


def pool_mix(p, pool_past, pos0, pool_w, pool_scale):
    t = p.shape[1]
    ext = jnp.concatenate([pool_past, p], axis=1).astype(jnp.float32)
    csum = jnp.concatenate([jnp.zeros_like(ext[:, :1]), jnp.cumsum(ext, axis=1)], axis=1)
    pos = pos0 + jnp.arange(t)
    s = POOL_STATE
    outs = []
    for g, w in enumerate(POOL_WINDOWS):
        sl = slice(g * POOL_GROUP_DIM, (g + 1) * POOL_GROUP_DIM)
        win_sum = csum[:, s + 1:s + 1 + t, sl] - csum[:, s + 1 - w:s + 1 - w + t, sl]
        cnt = jnp.minimum(pos + 1, w).astype(jnp.float32)[None, :, None]
        d = win_sum / cnt - ext[:, s:, sl]
        outs.append(d @ pool_w[g].astype(jnp.float32))
    out = jnp.concatenate(outs, axis=-1) * pool_scale.astype(jnp.float32)
    return out.astype(p.dtype), ext[:, -POOL_STATE:].astype(p.dtype)


def layer(x, attend, pool_past, pos0, layer_idx,
          ffn1_pre, ffn1_post, ffn1_wg, ffn1_wu, ffn1_wd,
          mix_pre, mix_post, w_in, lam_q1, lam_k1, lam_q2, lam_k2, subln,
          pool_w, pool_scale, w_branch_a, w_branch_b, w_out,
          ffn2_pre, ffn2_post, ffn2_wg, ffn2_wu, ffn2_wd):
    b, t, _ = x.shape
    h = half_ffn(x, ffn1_pre, ffn1_post, ffn1_wg, ffn1_wu, ffn1_wd)
    u = rms_norm(h, mix_pre)
    proj = u @ w_in
    q, k, v, p, ga, gb = jnp.split(
        proj, [Q_W, Q_W + K_W, Q_W + K_W + V_W, Q_W + K_W + V_W + POOL_WIDTH,
               Q_W + K_W + V_W + POOL_WIDTH + D_MODEL], axis=-1)
    q = q.reshape(b, t, N_HEADS, QK_DIM)
    k = k.reshape(b, t, N_HEADS, QK_DIM)
    v = v.reshape(b, t, N_HEADS, V_DIM)
    lam_init = 0.8 - 0.6 * math.exp(-0.3 * layer_idx)
    lam = (jnp.exp(jnp.sum(lam_q1.astype(jnp.float32) * lam_k1.astype(jnp.float32)))
           - jnp.exp(jnp.sum(lam_q2.astype(jnp.float32) * lam_k2.astype(jnp.float32)))
           + lam_init)
    o = attend(q, k, v, lam)
    o = (rms_norm(o, subln) * (1.0 - lam_init)).reshape(b, t, ATTN_WIDTH).astype(u.dtype)
    p_out, pool_new = pool_mix(p, pool_past, pos0, pool_w, pool_scale)
    merged = jax.nn.sigmoid(ga) * (o @ w_branch_a) + jax.nn.sigmoid(gb) * (p_out @ w_branch_b)
    h2 = h + rms_norm(merged @ w_out, mix_post)
    y = half_ffn(h2, ffn2_pre, ffn2_post, ffn2_wg, ffn2_wu, ffn2_wd)
    return y, k, v, pool_new


def setup_inputs(seed: int = 0) -> dict:
    key = jax.random.key(seed)
    ks = jax.random.split(key, 40)
    n_pages = PAST_LEN // PAGE_SIZE
    n_used = DEC_BATCH * n_pages
    n_phys = n_used + max(1, n_used // 4)

    def w(k, shape, fan_in):
        return jax.random.normal(k, shape, jnp.float32) * fan_in ** -0.5

    def gain(k, shape):
        return 1.0 + 0.05 * jax.random.normal(k, shape, jnp.float32)

    page_table = jax.random.permutation(ks[0], n_phys)[:n_used].reshape(DEC_BATCH, n_pages).astype(jnp.int32)
    return {
        "x_prompt": jax.random.normal(ks[1], (BATCH, SEQ, D_MODEL), jnp.float32),
        "x_sample": jax.random.normal(ks[2], (DEC_BATCH, DEC_SEQ, D_MODEL), jnp.float32),
        "cache_k": jax.random.normal(ks[3], (DEPTH, n_phys, PAGE_SIZE, N_HEADS, QK_DIM), jnp.float32),
        "cache_v": jax.random.normal(ks[4], (DEPTH, n_phys, PAGE_SIZE, N_HEADS, V_DIM), jnp.float32),
        "state_pool": jax.random.normal(ks[5], (DEPTH, DEC_BATCH, POOL_STATE, POOL_WIDTH), jnp.float32),
        "page_table": page_table,
        "rel_bias": 0.5 * jax.random.normal(ks[6], (N_BUCKETS, N_HEADS), jnp.float32),
        "ffn1_norm_pre": gain(ks[7], (DEPTH, D_MODEL)),
        "ffn1_norm_post": gain(ks[8], (DEPTH, D_MODEL)),
        "ffn1_w_gate": w(ks[9], (DEPTH, D_MODEL, D_FF), D_MODEL),
        "ffn1_w_up": w(ks[10], (DEPTH, D_MODEL, D_FF), D_MODEL),
        "ffn1_w_down": w(ks[11], (DEPTH, D_FF, D_MODEL), D_FF),
        "mix_norm_pre": gain(ks[12], (DEPTH, D_MODEL)),
        "mix_norm_post": gain(ks[13], (DEPTH, D_MODEL)),
        "w_in": w(ks[14], (DEPTH, D_MODEL, IN_WIDTH), D_MODEL),
        "lambda_q1": 0.1 * jax.random.normal(ks[15], (DEPTH, HEAD_DIM), jnp.float32),
        "lambda_k1": 0.1 * jax.random.normal(ks[16], (DEPTH, HEAD_DIM), jnp.float32),
        "lambda_q2": 0.1 * jax.random.normal(ks[17], (DEPTH, HEAD_DIM), jnp.float32),
        "lambda_k2": 0.1 * jax.random.normal(ks[18], (DEPTH, HEAD_DIM), jnp.float32),
        "attn_subln": gain(ks[19], (DEPTH, V_DIM)),
        "pool_w": w(ks[20], (DEPTH, POOL_GROUPS, POOL_GROUP_DIM, POOL_GROUP_DIM), POOL_GROUP_DIM),
        "pool_scale": gain(ks[21], (DEPTH, POOL_WIDTH)),
        "w_branch_a": w(ks[22], (DEPTH, ATTN_WIDTH, D_MODEL), ATTN_WIDTH),
        "w_branch_b": w(ks[23], (DEPTH, POOL_WIDTH, D_MODEL), POOL_WIDTH),
        "w_out": w(ks[24], (DEPTH, D_MODEL, D_MODEL), D_MODEL),
        "ffn2_norm_pre": gain(ks[25], (DEPTH, D_MODEL)),
        "ffn2_norm_post": gain(ks[26], (DEPTH, D_MODEL)),
        "ffn2_w_gate": w(ks[27], (DEPTH, D_MODEL, D_FF), D_MODEL),
        "ffn2_w_up": w(ks[28], (DEPTH, D_MODEL, D_FF), D_MODEL),
        "ffn2_w_down": w(ks[29], (DEPTH, D_FF, D_MODEL), D_FF),
    }


def reference(x_prompt, x_sample, cache_k, cache_v, state_pool, page_table, rel_bias,
              ffn1_norm_pre, ffn1_norm_post, ffn1_w_gate, ffn1_w_up, ffn1_w_down,
              mix_norm_pre, mix_norm_post, w_in, lambda_q1, lambda_k1, lambda_q2, lambda_k2,
              attn_subln, pool_w, pool_scale, w_branch_a, w_branch_b, w_out,
              ffn2_norm_pre, ffn2_norm_post, ffn2_w_gate, ffn2_w_up, ffn2_w_down):
    n_dec = x_sample.shape[0]
    past_len = page_table.shape[1] * PAGE_SIZE
    xp, xs = x_prompt, x_sample
    pool_pad = jnp.zeros((x_prompt.shape[0], POOL_STATE, POOL_WIDTH), x_prompt.dtype)
    attend_p = functools.partial(attend_prompt, rel_bias=rel_bias)
    kp_l, vp_l, pp_l, ks_l, vs_l, ps_l = [], [], [], [], [], []
    for l in range(DEPTH):
        lp = (ffn1_norm_pre[l], ffn1_norm_post[l], ffn1_w_gate[l], ffn1_w_up[l], ffn1_w_down[l],
              mix_norm_pre[l], mix_norm_post[l], w_in[l],
              lambda_q1[l], lambda_k1[l], lambda_q2[l], lambda_k2[l], attn_subln[l],
              pool_w[l], pool_scale[l], w_branch_a[l], w_branch_b[l], w_out[l],
              ffn2_norm_pre[l], ffn2_norm_post[l], ffn2_w_gate[l], ffn2_w_up[l], ffn2_w_down[l])
        xp, kp, vp, pp = layer(xp, attend_p, pool_pad, 0, l, *lp)
        k_past = cache_k[l][page_table].reshape(n_dec, past_len, N_HEADS, QK_DIM)
        v_past = cache_v[l][page_table].reshape(n_dec, past_len, N_HEADS, V_DIM)
        attend_s = functools.partial(attend_sample, rel_bias=rel_bias, k_past=k_past, v_past=v_past)
        xs, ks, vs, ps = layer(xs, attend_s, state_pool[l], past_len, l, *lp)
        kp_l.append(kp); vp_l.append(vp); pp_l.append(pp)
        ks_l.append(ks); vs_l.append(vs); ps_l.append(ps)
    return (xp, xs, jnp.stack(kp_l), jnp.stack(vp_l), jnp.stack(pp_l),
            jnp.stack(ks_l), jnp.stack(vs_l), jnp.stack(ps_l))
```

```python
import functools
import math

import jax
import jax.numpy as jnp
from jax import lax
from jax.experimental import pallas as pl
from jax.experimental.pallas import tpu as pltpu

F32 = jnp.float32
BF16 = jnp.bfloat16

N_HEADS = 4
HEAD_DIM = 64
QK_DIM = 2 * HEAD_DIM
V_DIM = 2 * HEAD_DIM
ATTN_W = N_HEADS * V_DIM
POOL_WINDOWS = (2, 4, 8, 16)
POOL_STATE = max(POOL_WINDOWS) - 1
N_BUCKETS = 32
MAX_DISTANCE = 128
PAGE_SIZE = 128
NORM_EPS = 1e-6
NEG_INF = -1e30
SCALE = HEAD_DIM ** -0.5

LANES = 128
SUBLANES = 8
VMEM_LIMIT_BYTES = 56 * 1024 * 1024


def _rms(x, g):
    ms = jnp.mean(x * x, axis=-1, keepdims=True)
    return x * lax.rsqrt(ms + NORM_EPS) * g


def _const_spec(shape):
    nd = len(shape)
    return pl.BlockSpec(shape, lambda *_: (0,) * nd, pipeline_mode=pl.Buffered(1))


def _swiglu(u, wgu_ref, wd_ref, d_ff):
    gu = jnp.dot(u, wgu_ref[...], preferred_element_type=F32)
    g = gu[:, :d_ff]
    act = (g * jax.nn.sigmoid(g) * gu[:, d_ff:]).astype(BF16)
    return jnp.dot(act, wd_ref[...], preferred_element_type=F32)


def _ffn_in_kernel(x_ref, pre_ref, post_ref, wgu_ref, wd_ref, mpre_ref, win_ref, *out_refs,
                   d_ff, transposed, tk):
    x = x_ref[...]
    f = _swiglu(_rms(x, pre_ref[...]).astype(BF16), wgu_ref, wd_ref, d_ff)
    h = x + 0.5 * _rms(f, post_ref[...])
    u = _rms(h, mpre_ref[...]).astype(BF16)
    proj = jnp.dot(u, win_ref[...], preferred_element_type=F32)
    w = ATTN_W
    q = proj[:, :w] * SCALE
    k = proj[:, w:2 * w]
    v = proj[:, 2 * w:3 * w]
    p = proj[:, 3 * w:]
    if transposed:
        h_ref, k_ref, v_ref, p_ref, qt_ref, kb_ref, vt_ref = out_refs
        qt_ref[...] = q.T.astype(BF16)
        kb_ref[...] = k.astype(BF16)
        vt = v.T.astype(BF16)
        for j in range(vt_ref.shape[0]):
            vt_ref[j] = vt[:, j * tk:(j + 1) * tk]
    else:
        h_ref, k_ref, v_ref, p_ref, q_ref = out_refs
        q_ref[...] = q
    h_ref[...] = h
    k_ref[...] = k
    v_ref[...] = v
    p_ref[...] = p


def _ffn_in(x, pre, post, wgu, wd, mpre, win, *, tm, transposed, tk):
    b, t, d = x.shape
    d_ff = wd.shape[0]
    w = ATTN_W
    grid = (b, t // tm)
    tok = lambda width: pl.BlockSpec((None, tm, width), lambda bi, i: (bi, i, 0))
    out_shape = [jax.ShapeDtypeStruct((b, t, d), F32)] + [jax.ShapeDtypeStruct((b, t, w), F32)] * 3
    out_specs = [tok(d), tok(w), tok(w), tok(w)]
    if transposed:
        nk = tm // tk
        out_shape += [jax.ShapeDtypeStruct((b, w, t), BF16),
                      jax.ShapeDtypeStruct((b, t, w), BF16),
                      jax.ShapeDtypeStruct((b, t // tk, w, tk), BF16)]
        out_specs += [pl.BlockSpec((None, w, tm), lambda bi, i: (bi, 0, i)),
                      tok(w),
                      pl.BlockSpec((None, nk, w, tk), lambda bi, i: (bi, i, 0, 0))]
    else:
        out_shape += [jax.ShapeDtypeStruct((b, t, w), F32)]
        out_specs += [tok(w)]
    return pl.pallas_call(
        functools.partial(_ffn_in_kernel, d_ff=d_ff, transposed=transposed, tk=tk),
        grid=grid,
        in_specs=[tok(d), _const_spec(pre.shape), _const_spec(post.shape), _const_spec(wgu.shape),
                  _const_spec(wd.shape), _const_spec(mpre.shape), _const_spec(win.shape)],
        out_specs=out_specs,
        out_shape=out_shape,
        compiler_params=pltpu.CompilerParams(
            dimension_semantics=("parallel", "parallel"), vmem_limit_bytes=VMEM_LIMIT_BYTES),
        name="ffn_in_prompt" if transposed else "ffn_in_sample",
    )(x, pre, post, wgu, wd, mpre, win)


def _lambda_value(lamv, lam_init):
    s1 = jnp.sum(lamv[0:1] * lamv[1:2], axis=1, keepdims=True)
    s2 = jnp.sum(lamv[2:3] * lamv[3:4], axis=1, keepdims=True)
    return jnp.exp(s1) - jnp.exp(s2) + lam_init


def _bias_by_distance(rel_bias, n):
    dist = jnp.arange(n, dtype=jnp.int32)
    max_exact = N_BUCKETS // 2
    nf = jnp.maximum(dist, 1).astype(F32)
    large = max_exact + (jnp.log(nf / max_exact) / math.log(MAX_DISTANCE / max_exact)
                         * (N_BUCKETS - max_exact)).astype(jnp.int32)
    large = jnp.minimum(large, N_BUCKETS - 1)
    bucket = jnp.where(dist < max_exact, dist, large)
    return rel_bias[bucket].T.astype(F32)


def _attn_prompt_kernel(qt_ref, k_ref, vt_ref, bias_ref, lamv_ref, subln_ref, o_ref,
                        m_ref, l_ref, acc_ref, *, tq, tk, lam_init):
    qi = pl.program_id(2)
    ratio = tq // tk
    n_special = bias_ref.shape[0]

    qt = qt_ref[...]
    row = lax.broadcasted_iota(jnp.int32, qt.shape, 0)
    zero = jnp.zeros_like(qt)
    q2 = jnp.concatenate([jnp.where(row < HEAD_DIM, qt, zero),
                          jnp.where(row >= HEAD_DIM, qt, zero)], axis=1)

    m_ref[...] = jnp.full(m_ref.shape, NEG_INF, F32)
    l_ref[...] = jnp.zeros(l_ref.shape, F32)
    acc_ref[...] = jnp.zeros(acc_ref.shape, F32)

    def step(kj, bias):
        start = pl.multiple_of(kj * tk, tk)
        s = jnp.dot(k_ref[pl.ds(start, tk), :], q2, preferred_element_type=F32)
        if bias is not None:
            s = s + jnp.concatenate([bias, bias], axis=1)
        m_old = m_ref[...]
        m_new = jnp.maximum(m_old, jnp.max(s, axis=0, keepdims=True))
        alpha = jnp.exp(m_old - m_new)
        p = jnp.exp(s - m_new)
        l_ref[...] = alpha * l_ref[...] + jnp.sum(p, axis=0, keepdims=True)
        pv = jnp.dot(vt_ref[kj], p.astype(BF16), preferred_element_type=F32)
        acc_ref[...] = alpha * acc_ref[...] + pv
        m_ref[...] = m_new

    first_special = qi * ratio - 1

    def fast_body(kj, carry):
        step(kj, None)
        return carry

    lax.fori_loop(0, jnp.maximum(first_special, 0), fast_body, 0)

    @pl.when(qi > 0)
    def _():
        step(first_special, bias_ref[0])

    for j in range(1, n_special):
        step(first_special + j, bias_ref[j])

    l = l_ref[...]
    acc = acc_ref[...]
    lam = _lambda_value(lamv_ref[...], lam_init)
    o = acc[:, :tq] / l[:, :tq] - lam * (acc[:, tq:] / l[:, tq:])
    ms = jnp.mean(o * o, axis=0, keepdims=True)
    y = o * lax.rsqrt(ms + NORM_EPS) * subln_ref[...] * (1.0 - lam_init)
    o_ref[...] = y.T.astype(BF16)


def _prompt_bias_tiles(bias_tab, tq, tk):
    n_special = tq // tk + 1
    j = jnp.arange(n_special, dtype=jnp.int32)[:, None, None]
    c = jnp.arange(tk, dtype=jnp.int32)[None, :, None]
    r = jnp.arange(tq, dtype=jnp.int32)[None, None, :]
    dist = r - c + (1 - j) * tk
    shifted = bias_tab - bias_tab[:, -1:]
    vals = shifted[:, jnp.clip(dist, 0, bias_tab.shape[1] - 1)]
    return jnp.where(dist[None] < 0, NEG_INF, vals).astype(F32)


def _attn_prompt(qt, kb, vt, bias_tiles, lamv, subln_col, *, tq, tk, lam_init):
    b, w, t = qt.shape
    nk = t // tk
    n_special = bias_tiles.shape[1]
    grid = (b, N_HEADS, t // tq)
    return pl.pallas_call(
        functools.partial(_attn_prompt_kernel, tq=tq, tk=tk, lam_init=lam_init),
        grid=grid,
        in_specs=[
            pl.BlockSpec((None, V_DIM, tq), lambda bi, h, qi: (bi, h, qi)),
            pl.BlockSpec((None, t, QK_DIM), lambda bi, h, qi: (bi, 0, h)),
            pl.BlockSpec((None, nk, V_DIM, tk), lambda bi, h, qi: (bi, 0, h, 0)),
            pl.BlockSpec((None, n_special, tk, tq), lambda bi, h, qi: (h, 0, 0, 0)),
            pl.BlockSpec(lamv.shape, lambda bi, h, qi: (0, 0)),
            pl.BlockSpec(subln_col.shape, lambda bi, h, qi: (0, 0)),
        ],
        out_specs=pl.BlockSpec((None, tq, V_DIM), lambda bi, h, qi: (bi, qi, h)),
        out_shape=jax.ShapeDtypeStruct((b, t, w), BF16),
        scratch_shapes=[pltpu.VMEM((1, 2 * tq), F32), pltpu.VMEM((1, 2 * tq), F32),
                        pltpu.VMEM((V_DIM, 2 * tq), F32)],
        compiler_params=pltpu.CompilerParams(
            dimension_semantics=("parallel", "parallel", "arbitrary"),
            vmem_limit_bytes=VMEM_LIMIT_BYTES),
        name="attn_prompt",
    )(qt, kb, vt, bias_tiles, lamv, subln_col)


def _attn_sample_kernel(pt_ref, q_ref, kn_ref, vn_ref, bias_ref, bias_self_ref, lamv_ref,
                        subln_ref, *rest, n_pages, lam_init):
    del pt_ref
    kp_refs = rest[:n_pages]
    vp_refs = rest[n_pages:2 * n_pages]
    o_ref = rest[2 * n_pages]
    nt = (((1,), (1,)), ((), ()))
    tn = (((0,), (0,)), ((), ()))
    n_col = 2 * N_HEADS
    width = N_HEADS * QK_DIM

    rowj = lax.broadcasted_iota(jnp.int32, (n_col, width), 0)
    lane = lax.broadcasted_iota(jnp.int32, (n_col, width), 1)
    seg_lo = jnp.where(rowj < N_HEADS, rowj * QK_DIM, (rowj - N_HEADS) * QK_DIM + HEAD_DIM)
    in_seg = (lane >= seg_lo) & (lane < seg_lo + HEAD_DIM)
    qm = jnp.where(in_seg, jnp.broadcast_to(q_ref[...], (n_col, width)), 0.0)

    s_pages = [lax.dot_general(kp[...], qm, nt, preferred_element_type=F32) for kp in kp_refs]
    s = jnp.concatenate(s_pages, axis=0) + bias_ref[...]
    kn = jnp.broadcast_to(kn_ref[...], (SUBLANES, width))
    s_self = lax.dot_general(kn, qm, nt, preferred_element_type=F32) + bias_self_ref[...]

    m = jnp.maximum(jnp.max(s, axis=0, keepdims=True), jnp.max(s_self, axis=0, keepdims=True))
    p = jnp.exp(s - m)
    p_self = jnp.exp(s_self - m)
    l = jnp.sum(p, axis=0, keepdims=True) + jnp.sum(p_self, axis=0, keepdims=True)
    lam = _lambda_value(lamv_ref[...], lam_init)
    col = lax.broadcasted_iota(jnp.int32, l.shape, 1)
    coef = jnp.where(col < N_HEADS, 1.0, -lam) / l
    pc = p * coef
    pc_self = p_self * coef

    acc = lax.dot_general(pc_self, jnp.broadcast_to(vn_ref[...], (SUBLANES, width)), tn,
                          preferred_element_type=F32)
    for j, vp in enumerate(vp_refs):
        acc = acc + lax.dot_general(pc[j * PAGE_SIZE:(j + 1) * PAGE_SIZE], vp[...], tn,
                                    preferred_element_type=F32)
    head_of_lane = lane // V_DIM
    row_head = jnp.where(rowj < N_HEADS, rowj, rowj - N_HEADS)
    o = jnp.sum(jnp.where(head_of_lane == row_head, acc, 0.0), axis=0, keepdims=True)

    hrow = lax.broadcasted_iota(jnp.int32, (N_HEADS, width), 0)
    hlane = lax.broadcasted_iota(jnp.int32, (N_HEADS, width), 1) // V_DIM
    hmask = hrow == hlane
    ms = jnp.sum(jnp.where(hmask, o * o, 0.0), axis=1, keepdims=True) / V_DIM
    inv = jnp.sum(jnp.where(hmask, lax.rsqrt(ms + NORM_EPS), 0.0), axis=0, keepdims=True)
    o_ref[...] = o * inv * subln_ref[...] * (1.0 - lam_init)


def _attn_sample(page_table, q, k_new, v_new, cache_k, cache_v, bias_past, bias_self, lamv,
                 subln_row, *, layer, lam_init):
    n_seq, n_pages = page_table.shape
    width = q.shape[-1]
    row_spec = pl.BlockSpec((None, 1, width), lambda s, pt: (s, 0, 0))
    const2 = lambda a: pl.BlockSpec(a.shape, lambda s, pt: (0, 0))

    def page_spec(j):
        return pl.BlockSpec((None, None, PAGE_SIZE, width),
                            lambda s, pt: (layer, pt[s * n_pages + j], 0, 0))

    grid_spec = pltpu.PrefetchScalarGridSpec(
        num_scalar_prefetch=1,
        grid=(n_seq,),
        in_specs=[row_spec, row_spec, row_spec, const2(bias_past), const2(bias_self), const2(lamv),
                  const2(subln_row)]
                 + [page_spec(j) for j in range(n_pages)] * 2,
        out_specs=row_spec,
    )
    return pl.pallas_call(
        functools.partial(_attn_sample_kernel, n_pages=n_pages, lam_init=lam_init),
        grid_spec=grid_spec,
        out_shape=jax.ShapeDtypeStruct(q.shape, F32),
        compiler_params=pltpu.CompilerParams(
            dimension_semantics=("arbitrary",), vmem_limit_bytes=VMEM_LIMIT_BYTES),
        name="attn_sample",
    )(page_table.reshape(-1), q, k_new, v_new, bias_past, bias_self, lamv, subln_row,
      *([cache_k] * n_pages), *([cache_v] * n_pages))


def _pool_prompt_kernel(p_ref, halo_ref, d_ref, ext_ref, *, tp):
    i = pl.program_id(1)
    halo = POOL_STATE + 1
    p = p_ref[...]
    ext_ref[pl.ds(0, halo), :] = jnp.where(i > 0, halo_ref[...], 0.0)
    ext_ref[pl.ds(halo, tp), :] = p
    gd = p.shape[1] // len(POOL_WINDOWS)
    pos1 = i * tp + lax.broadcasted_iota(jnp.int32, (tp, gd), 0) + 1
    for g, w in enumerate(POOL_WINDOWS):
        lanes = pl.ds(g * gd, gd)
        ws = ext_ref[pl.ds(halo, tp), lanes]
        for j in range(1, w):
            ws = ws + ext_ref[pl.ds(halo - j, tp), lanes]
        cnt = jnp.minimum(pos1, w).astype(F32)
        d_ref[:, lanes] = (ws / cnt - p[:, g * gd:(g + 1) * gd]).astype(d_ref.dtype)


def _pool_prompt(p, *, tp):
    b, t, w = p.shape
    halo = POOL_STATE + 1
    per = tp // halo
    return pl.pallas_call(
        functools.partial(_pool_prompt_kernel, tp=tp),
        grid=(b, t // tp),
        in_specs=[pl.BlockSpec((None, tp, w), lambda bi, i: (bi, i, 0)),
                  pl.BlockSpec((None, halo, w), lambda bi, i: (bi, jnp.maximum(i * per - 1, 0), 0))],
        out_specs=pl.BlockSpec((None, tp, w), lambda bi, i: (bi, i, 0)),
        out_shape=jax.ShapeDtypeStruct(p.shape, BF16),
        scratch_shapes=[pltpu.VMEM((tp + halo, w), F32)],
        compiler_params=pltpu.CompilerParams(dimension_semantics=("parallel", "parallel")),
        name="pool_prompt",
    )(p, p)


def _pool_sample_kernel(state_ref, p_ref, d_ref, new_ref, *, past_len):
    w_tot = p_ref.shape[1]
    gd = w_tot // len(POOL_WINDOWS)
    p = p_ref[...]
    for g, w in enumerate(POOL_WINDOWS):
        ws = p[:, g * gd:(g + 1) * gd]
        for j in range(1, w):
            row = POOL_STATE - j
            ws = ws + state_ref[:, pl.ds(row * w_tot + g * gd, gd)]
        cnt = float(min(past_len + 1, w))
        d_ref[:, pl.ds(g * gd, gd)] = (ws / cnt - p[:, g * gd:(g + 1) * gd]).astype(d_ref.dtype)
    keep = (POOL_STATE - 1) * w_tot
    new_ref[:, pl.ds(0, keep)] = state_ref[:, pl.ds(w_tot, keep)]
    new_ref[:, pl.ds(keep, w_tot)] = p


def _pool_sample(state, p, *, past_len):
    return pl.pallas_call(
        functools.partial(_pool_sample_kernel, past_len=past_len),
        out_shape=[jax.ShapeDtypeStruct(p.shape, BF16), jax.ShapeDtypeStruct(state.shape, F32)],
        name="pool_sample",
    )(state, p)


def _merge_out_kernel(h_ref, o_ref, d_ref, mpre_ref, wgate_ref, poolw_ref, pscale_ref, wa_ref,
                      wb_ref, wout_ref, mpost_ref, pre2_ref, post2_ref, wgu_ref, wd_ref, y_ref,
                      *, d_ff):
    h = h_ref[...]
    d_model = h.shape[1]
    u = _rms(h, mpre_ref[...]).astype(BF16)
    gates = jnp.dot(u, wgate_ref[...], preferred_element_type=F32)
    d = d_ref[...]
    gd = d.shape[1] // len(POOL_WINDOWS)
    pooled = jnp.concatenate(
        [jnp.dot(d[:, g * gd:(g + 1) * gd], poolw_ref[g], preferred_element_type=F32)
         for g in range(len(POOL_WINDOWS))], axis=1) * pscale_ref[...]
    branch_a = jnp.dot(o_ref[...], wa_ref[...], preferred_element_type=F32)
    branch_b = jnp.dot(pooled.astype(BF16), wb_ref[...], preferred_element_type=F32)
    merged = (jax.nn.sigmoid(gates[:, :d_model]) * branch_a
              + jax.nn.sigmoid(gates[:, d_model:]) * branch_b)
    mixed = jnp.dot(merged.astype(BF16), wout_ref[...], preferred_element_type=F32)
    h2 = h + _rms(mixed, mpost_ref[...])
    f = _swiglu(_rms(h2, pre2_ref[...]).astype(BF16), wgu_ref, wd_ref, d_ff)
    y_ref[...] = h2 + 0.5 * _rms(f, post2_ref[...])


def _merge_out(h, o, d, weights, *, tm, name):
    b, t, dm = h.shape
    tok = lambda width: pl.BlockSpec((None, tm, width), lambda bi, i: (bi, i, 0))
    d_ff = weights[-1].shape[0]
    return pl.pallas_call(
        functools.partial(_merge_out_kernel, d_ff=d_ff),
        grid=(b, t // tm),
        in_specs=[tok(dm), tok(o.shape[-1]), tok(d.shape[-1])] + [_const_spec(w.shape) for w in weights],
        out_specs=tok(dm),
        out_shape=jax.ShapeDtypeStruct(h.shape, F32),
        compiler_params=pltpu.CompilerParams(
            dimension_semantics=("parallel", "parallel"), vmem_limit_bytes=VMEM_LIMIT_BYTES),
        name=name,
    )(h, o, d, *weights)


TM_PROMPT = 256
TQ = 512
TK = 256
TP = 512


def kernel(x_prompt, x_sample, cache_k, cache_v, state_pool, page_table, rel_bias, ffn1_norm_pre, ffn1_norm_post, ffn1_w_gate, ffn1_w_up, ffn1_w_down, mix_norm_pre, mix_norm_post, w_in, lambda_q1, lambda_k1, lambda_q2, lambda_k2, attn_subln, pool_w, pool_scale, w_branch_a, w_branch_b, w_out, ffn2_norm_pre, ffn2_norm_post, ffn2_w_gate, ffn2_w_up, ffn2_w_down):
    depth = w_in.shape[0]
    bsz, seq, d_model = x_prompt.shape
    n_seq = x_sample.shape[0]
    n_pages = page_table.shape[1]
    past_len = n_pages * PAGE_SIZE
    n_phys = cache_k.shape[1]
    w = ATTN_W
    qkvp_w = 4 * w

    wgu1 = jnp.concatenate([ffn1_w_gate, ffn1_w_up], axis=-1).astype(BF16)
    wgu2 = jnp.concatenate([ffn2_w_gate, ffn2_w_up], axis=-1).astype(BF16)
    wd1 = ffn1_w_down.astype(BF16)
    wd2 = ffn2_w_down.astype(BF16)
    w_qkvp = w_in[:, :, :qkvp_w].astype(BF16)
    w_gate = w_in[:, :, qkvp_w:].astype(BF16)
    wa = w_branch_a.astype(BF16)
    wb = w_branch_b.astype(BF16)
    wo = w_out.astype(BF16)
    pw = pool_w.astype(BF16)
    row = lambda a: a[:, None, :]
    lamv = jnp.stack([lambda_q1, lambda_k1, lambda_q2, lambda_k2], axis=1)

    bias_tab = _bias_by_distance(rel_bias, max(TQ + TK, past_len + 1))
    bias_tiles = _prompt_bias_tiles(bias_tab[:, :TQ + TK], TQ, TK)
    past = bias_tab[:, 1:past_len + 1][:, ::-1].T
    bias_past = jnp.concatenate([past, past], axis=1)
    self_row = jnp.concatenate([bias_tab[:, 0], bias_tab[:, 0]])[None]
    bias_self = jnp.concatenate(
        [self_row, jnp.full((SUBLANES - 1, 2 * N_HEADS), NEG_INF, F32)], axis=0)

    ck = cache_k.reshape(depth, n_phys, PAGE_SIZE, w)
    cv = cache_v.reshape(depth, n_phys, PAGE_SIZE, w)

    xp = x_prompt
    xs = x_sample.reshape(1, n_seq, d_model)
    outs = [[] for _ in range(6)]
    for l in range(depth):
        lam_init = 0.8 - 0.6 * math.exp(-0.3 * l)
        in_w = (row(ffn1_norm_pre)[l], row(ffn1_norm_post)[l], wgu1[l], wd1[l],
                row(mix_norm_pre)[l], w_qkvp[l])
        out_w = (row(mix_norm_pre)[l], w_gate[l], pw[l], row(pool_scale)[l], wa[l], wb[l], wo[l],
                 row(mix_norm_post)[l], row(ffn2_norm_pre)[l], row(ffn2_norm_post)[l], wgu2[l], wd2[l])

        h, k, v, p, qt, kb, vt = _ffn_in(xp, *in_w, tm=TM_PROMPT, transposed=True, tk=TK)
        o = _attn_prompt(qt, kb, vt, bias_tiles, lamv[l], attn_subln[l][:, None],
                         tq=TQ, tk=TK, lam_init=lam_init)
        d = _pool_prompt(p, tp=TP)
        xp = _merge_out(h, o, d, out_w, tm=TM_PROMPT, name="merge_out_prompt")
        outs[0].append(k.reshape(bsz, seq, N_HEADS, QK_DIM))
        outs[1].append(v.reshape(bsz, seq, N_HEADS, V_DIM))
        outs[2].append(p[:, seq - POOL_STATE:, :])

        hs, ks, vs, ps, qs = _ffn_in(xs, *in_w, tm=n_seq, transposed=False, tk=TK)
        seq_rows = lambda a: a.reshape(n_seq, 1, w)
        o_s = _attn_sample(page_table, seq_rows(qs), seq_rows(ks), seq_rows(vs), ck, cv,
                           bias_past, bias_self, lamv[l], jnp.tile(attn_subln[l], N_HEADS)[None],
                           layer=l, lam_init=lam_init)
        d_s, new_state = _pool_sample(state_pool[l].reshape(n_seq, POOL_STATE * w),
                                      ps.reshape(n_seq, w), past_len=past_len)
        xs = _merge_out(hs, o_s.reshape(1, n_seq, w).astype(BF16), d_s.reshape(1, n_seq, w), out_w,
                        tm=n_seq, name="merge_out_sample")
        outs[3].append(ks.reshape(n_seq, 1, N_HEADS, QK_DIM))
        outs[4].append(vs.reshape(n_seq, 1, N_HEADS, V_DIM))
        outs[5].append(new_state.reshape(n_seq, POOL_STATE, w))

    stacked = [jnp.stack(o) for o in outs]
    return (xp, xs.reshape(n_seq, 1, d_model), stacked[0], stacked[1], stacked[2],
            stacked[3], stacked[4], stacked[5])
```

```python
import functools
import math

import jax
import jax.numpy as jnp
from jax import lax
from jax.experimental import pallas as pl
from jax.experimental.pallas import tpu as pltpu

F32 = jnp.float32
BF16 = jnp.bfloat16

N_HEADS = 4
HEAD_DIM = 64
QK_DIM = 2 * HEAD_DIM
V_DIM = 2 * HEAD_DIM
ATTN_W = N_HEADS * V_DIM
POOL_WINDOWS = (2, 4, 8, 16)
POOL_STATE = max(POOL_WINDOWS) - 1
N_BUCKETS = 32
MAX_DISTANCE = 128
PAGE_SIZE = 128
NORM_EPS = 1e-6
NEG_INF = -1e30
SCALE = HEAD_DIM ** -0.5
LOG2E = math.log2(math.e)

LANES = 128
SUBLANES = 8
VMEM_LIMIT_BYTES = 56 * 1024 * 1024


def _rms(x, g):
    ms = jnp.mean(x * x, axis=-1, keepdims=True)
    return x * lax.rsqrt(ms + NORM_EPS) * g


def _const_spec(shape):
    nd = len(shape)
    return pl.BlockSpec(shape, lambda *_: (0,) * nd, pipeline_mode=pl.Buffered(1))


def _swiglu(u, wgu_ref, wd_ref, d_ff):
    gu = jnp.dot(u, wgu_ref[...], preferred_element_type=F32)
    g = gu[:, :d_ff]
    act = (g * jax.nn.sigmoid(g) * gu[:, d_ff:]).astype(BF16)
    return jnp.dot(act, wd_ref[...], preferred_element_type=F32)


def _ffn_in_body(x_ref, pre_ref, post_ref, wgu_ref, wd_ref, mpre_ref, win_ref, d_ff):
    x = x_ref[...]
    f = _swiglu(_rms(x, pre_ref[...]).astype(BF16), wgu_ref, wd_ref, d_ff)
    h = x + 0.5 * _rms(f, post_ref[...])
    u = _rms(h, mpre_ref[...]).astype(BF16)
    proj = jnp.dot(u, win_ref[...], preferred_element_type=F32)
    w = ATTN_W
    return h, proj[:, :w], proj[:, w:2 * w], proj[:, 2 * w:3 * w], proj[:, 3 * w:]


def _ffn_in_prompt_kernel(x_ref, pre_ref, post_ref, wgu_ref, wd_ref, mpre_ref, win_ref,
                          kbuf_ref, vbuf_ref, h_ref, p_ref, qt_ref, kb_ref, vt_ref, k5_ref, v5_ref,
                          *, d_ff, tk):
    del kbuf_ref, vbuf_ref
    h, q, k, v, p = _ffn_in_body(x_ref, pre_ref, post_ref, wgu_ref, wd_ref, mpre_ref, win_ref, d_ff)
    h_ref[...] = h
    p_ref[...] = p
    qt_ref[...] = (q * (SCALE * LOG2E)).T.astype(BF16)
    kb_ref[...] = k.astype(BF16)
    vt = v.T.astype(BF16)
    for j in range(vt_ref.shape[0]):
        vt_ref[j] = vt[:, j * tk:(j + 1) * tk]
    for hd in range(N_HEADS):
        k5_ref[:, hd, :] = k[:, hd * QK_DIM:(hd + 1) * QK_DIM]
        v5_ref[:, hd, :] = v[:, hd * V_DIM:(hd + 1) * V_DIM]


def _ffn_in_prompt(x, weights, kbuf, vbuf, *, layer, tm, tk):
    b, t, d = x.shape
    d_ff = weights[3].shape[0]
    w = ATTN_W
    tok = lambda width: pl.BlockSpec((None, tm, width), lambda bi, i: (bi, i, 0))
    kv5 = pl.BlockSpec((None, None, tm, N_HEADS, QK_DIM), lambda bi, i: (layer, bi, i, 0, 0))
    any_spec = pl.BlockSpec(memory_space=pl.ANY)
    n_in = 1 + len(weights)
    return pl.pallas_call(
        functools.partial(_ffn_in_prompt_kernel, d_ff=d_ff, tk=tk),
        grid=(b, t // tm),
        in_specs=[tok(d)] + [_const_spec(a.shape) for a in weights] + [any_spec, any_spec],
        out_specs=[tok(d), tok(w),
                   pl.BlockSpec((None, w, tm), lambda bi, i: (bi, 0, i)),
                   tok(w),
                   pl.BlockSpec((None, tm // tk, w, tk), lambda bi, i: (bi, i, 0, 0)),
                   kv5, kv5],
        out_shape=[jax.ShapeDtypeStruct((b, t, d), F32),
                   jax.ShapeDtypeStruct((b, t, w), F32),
                   jax.ShapeDtypeStruct((b, w, t), BF16),
                   jax.ShapeDtypeStruct((b, t, w), BF16),
                   jax.ShapeDtypeStruct((b, t // tk, w, tk), BF16),
                   jax.ShapeDtypeStruct(kbuf.shape, F32),
                   jax.ShapeDtypeStruct(vbuf.shape, F32)],
        input_output_aliases={n_in: 5, n_in + 1: 6},
        compiler_params=pltpu.CompilerParams(
            dimension_semantics=("parallel", "parallel"), vmem_limit_bytes=VMEM_LIMIT_BYTES),
        name="ffn_in_prompt",
    )(x, *weights, kbuf, vbuf)


def _ffn_in_sample_kernel(x_ref, pre_ref, post_ref, wgu_ref, wd_ref, mpre_ref, win_ref,
                          h_ref, q_ref, k_ref, v_ref, p_ref, *, d_ff):
    h, q, k, v, p = _ffn_in_body(x_ref, pre_ref, post_ref, wgu_ref, wd_ref, mpre_ref, win_ref, d_ff)
    h_ref[...] = h
    q_ref[...] = q * SCALE
    k_ref[...] = k
    v_ref[...] = v
    p_ref[...] = p


def _ffn_in_sample(x, weights):
    n, d = x.shape
    d_ff = weights[3].shape[0]
    return pl.pallas_call(
        functools.partial(_ffn_in_sample_kernel, d_ff=d_ff),
        out_shape=[jax.ShapeDtypeStruct((n, d), F32)] + [jax.ShapeDtypeStruct((n, ATTN_W), F32)] * 4,
        compiler_params=pltpu.CompilerParams(vmem_limit_bytes=VMEM_LIMIT_BYTES),
        name="ffn_in_sample",
    )(x, *weights)


def _lambda_value(lamv, lam_init):
    s1 = jnp.sum(lamv[0:1] * lamv[1:2], axis=1, keepdims=True)
    s2 = jnp.sum(lamv[2:3] * lamv[3:4], axis=1, keepdims=True)
    return jnp.exp(s1) - jnp.exp(s2) + lam_init


def _bias_by_distance(rel_bias, n):
    dist = jnp.arange(n, dtype=jnp.int32)
    max_exact = N_BUCKETS // 2
    nf = jnp.maximum(dist, 1).astype(F32)
    large = max_exact + (jnp.log(nf / max_exact) / math.log(MAX_DISTANCE / max_exact)
                         * (N_BUCKETS - max_exact)).astype(jnp.int32)
    large = jnp.minimum(large, N_BUCKETS - 1)
    bucket = jnp.where(dist < max_exact, dist, large)
    return rel_bias[bucket].T.astype(F32)


def _attn_prompt_kernel(qt_ref, k_ref, vt_ref, bias_ref, lamv_ref, subln_ref, o_ref,
                        q2_ref, m_ref, l_ref, acc_ref, *, tq, tk, qc, lam_init):
    qi = pl.program_id(2)
    ratio = tq // tk
    n_special = bias_ref.shape[0]
    n_chunks = 2 * tq // qc

    qt = qt_ref[...]
    row = lax.broadcasted_iota(jnp.int32, qt.shape, 0)
    zero = jnp.zeros_like(qt)
    q2_ref[:, pl.ds(0, tq)] = jnp.where(row < HEAD_DIM, qt, zero)
    q2_ref[:, pl.ds(tq, tq)] = jnp.where(row >= HEAD_DIM, qt, zero)
    m_ref[...] = jnp.full(m_ref.shape, NEG_INF, F32)
    l_ref[...] = jnp.zeros(l_ref.shape, F32)
    acc_ref[...] = jnp.zeros(acc_ref.shape, F32)

    def step(kj, special):
        start = pl.multiple_of(kj * tk, tk)
        k = k_ref[pl.ds(start, tk), :]
        vt = vt_ref[kj]
        for c in range(n_chunks):
            q_lo = (c * qc) % tq
            if special is not None:
                key_lo = (special - 1) * tk
                if q_lo + qc - 1 < key_lo:
                    continue
            cols = pl.ds(c * qc, qc)
            s = jnp.dot(k, q2_ref[:, cols], preferred_element_type=F32)
            if special is not None:
                s = s + bias_ref[special, :, pl.ds(q_lo, qc)]
            m_old = m_ref[:, cols]
            m_new = jnp.maximum(m_old, jnp.max(s, axis=0, keepdims=True))
            alpha = jnp.exp2(m_old - m_new)
            p = jnp.exp2(s - m_new)
            l_ref[:, cols] = alpha * l_ref[:, cols] + jnp.sum(p, axis=0, keepdims=True)
            pv = jnp.dot(vt, p.astype(BF16), preferred_element_type=F32)
            acc_ref[:, cols] = alpha * acc_ref[:, cols] + pv
            m_ref[:, cols] = m_new

    first_special = qi * ratio - 1

    def fast_group(g, carry):
        for t in range(ratio):
            step(g * ratio + t, None)
        return carry

    lax.fori_loop(0, jnp.maximum(qi - 1, 0), fast_group, 0)

    @pl.when(qi > 0)
    def _():
        for t in range(ratio - 1):
            step((qi - 1) * ratio + t, None)
        step(first_special, 0)

    for j in range(1, n_special):
        step(first_special + j, j)

    l = l_ref[...]
    acc = acc_ref[...]
    lam = _lambda_value(lamv_ref[...], lam_init)
    o = acc[:, :tq] / l[:, :tq] - lam * (acc[:, tq:] / l[:, tq:])
    ms = jnp.mean(o * o, axis=0, keepdims=True)
    y = o * lax.rsqrt(ms + NORM_EPS) * subln_ref[...] * (1.0 - lam_init)
    o_ref[...] = y.T.astype(BF16)


def _prompt_bias_tiles(bias_tab, tq, tk):
    n_special = tq // tk + 1
    rows = n_special * tk
    period = rows + tq
    x = jnp.arange(period, dtype=jnp.int32)
    dist = jnp.where(x < tq, x, x - period) + tk
    shifted = (bias_tab - bias_tab[:, -1:]) * LOG2E
    vals = jnp.where(dist < 0, NEG_INF, shifted[:, jnp.clip(dist, 0, bias_tab.shape[1] - 1)])
    flat = jnp.tile(vals, (1, rows))[:, :rows * (period - 1)]
    toeplitz = flat.reshape(-1, rows, period - 1)[:, :, :tq]
    return toeplitz.reshape(-1, n_special, tk, tq).astype(F32)


def _attn_prompt(qt, kb, vt, bias_tiles, lamv, subln_col, *, tq, tk, qc, lam_init):
    b, w, t = qt.shape
    nk = t // tk
    n_special = bias_tiles.shape[1]
    grid = (b, N_HEADS, t // tq)
    return pl.pallas_call(
        functools.partial(_attn_prompt_kernel, tq=tq, tk=tk, qc=qc, lam_init=lam_init),
        grid=grid,
        in_specs=[
            pl.BlockSpec((None, V_DIM, tq), lambda bi, h, qi: (bi, h, qi)),
            pl.BlockSpec((None, t, QK_DIM), lambda bi, h, qi: (bi, 0, h)),
            pl.BlockSpec((None, nk, V_DIM, tk), lambda bi, h, qi: (bi, 0, h, 0)),
            pl.BlockSpec((None, n_special, tk, tq), lambda bi, h, qi: (h, 0, 0, 0)),
            pl.BlockSpec(lamv.shape, lambda bi, h, qi: (0, 0)),
            pl.BlockSpec(subln_col.shape, lambda bi, h, qi: (0, 0)),
        ],
        out_specs=pl.BlockSpec((None, tq, V_DIM), lambda bi, h, qi: (bi, qi, h)),
        out_shape=jax.ShapeDtypeStruct((b, t, w), BF16),
        scratch_shapes=[pltpu.VMEM((QK_DIM, 2 * tq), BF16), pltpu.VMEM((1, 2 * tq), F32),
                        pltpu.VMEM((1, 2 * tq), F32), pltpu.VMEM((V_DIM, 2 * tq), F32)],
        compiler_params=pltpu.CompilerParams(
            dimension_semantics=("parallel", "parallel", "arbitrary"),
            vmem_limit_bytes=VMEM_LIMIT_BYTES),
        name="attn_prompt",
    )(qt, kb, vt, bias_tiles, lamv, subln_col)


def _attn_sample_kernel(pt_ref, q_ref, kn_ref, vn_ref, bias_ref, bias_self_ref, lamv_ref,
                        subln_ref, *rest, n_pages, lam_init):
    del pt_ref
    kp_refs = rest[:n_pages]
    vp_refs = rest[n_pages:2 * n_pages]
    o_ref = rest[2 * n_pages]
    nt = (((1,), (1,)), ((), ()))
    half = PAGE_SIZE * N_HEADS // 2
    n_col = 2 * N_HEADS

    qh = q_ref[...]
    lane = lax.broadcasted_iota(jnp.int32, qh.shape, 1)
    zero = jnp.zeros_like(qh)
    q8 = jnp.concatenate([jnp.where(lane < HEAD_DIM, qh, zero),
                          jnp.where(lane >= HEAD_DIM, qh, zero)], axis=0)
    z8 = jnp.zeros_like(q8)
    rt = jnp.concatenate([jnp.concatenate([q8, z8], axis=1),
                          jnp.concatenate([z8, q8], axis=1)], axis=0)

    def paired(ref):
        return jnp.concatenate([ref[pl.ds(0, half), :], ref[pl.ds(half, half), :]], axis=1)

    s = jnp.concatenate([lax.dot_general(paired(kp), rt, nt, preferred_element_type=F32)
                         for kp in kp_refs], axis=0) + bias_ref[...]
    kn = kn_ref[...]
    kn8 = jnp.concatenate([kn, kn], axis=0)
    s_self = lax.dot_general(jnp.concatenate([kn8, jnp.zeros_like(kn8)], axis=1), rt, nt,
                             preferred_element_type=F32) + bias_self_ref[...]

    m16 = jnp.maximum(jnp.max(s, axis=0, keepdims=True), jnp.max(s_self, axis=0, keepdims=True))
    m8 = jnp.maximum(m16[:, :n_col], m16[:, n_col:])
    m = jnp.concatenate([m8, m8], axis=1)
    p = jnp.exp(s - m)
    p_self = jnp.exp(s_self - m)
    l16 = jnp.sum(p, axis=0, keepdims=True) + jnp.sum(p_self, axis=0, keepdims=True)
    l8 = l16[:, :n_col] + l16[:, n_col:]
    lam = _lambda_value(lamv_ref[...], lam_init)
    col = lax.broadcasted_iota(jnp.int32, l8.shape, 1)
    coef8 = jnp.where(col < N_HEADS, 1.0, -lam) / l8
    coef = jnp.concatenate([coef8, coef8], axis=1)

    erow = lax.broadcasted_iota(jnp.int32, (2 * n_col, 2 * LANES), 0)
    elane = lax.broadcasted_iota(jnp.int32, (2 * n_col, 2 * LANES), 1)
    expand = ((erow < n_col) == (elane < LANES)).astype(F32)

    acc = jnp.zeros((half, V_DIM), F32)
    for j, vp in enumerate(vp_refs):
        wb = jnp.dot(p[j * half:(j + 1) * half] * coef, expand, preferred_element_type=F32)
        acc = acc + wb[:, :LANES] * vp[pl.ds(0, half), :] + wb[:, LANES:] * vp[pl.ds(half, half), :]
    acc8 = jnp.sum(acc.reshape(half // SUBLANES, SUBLANES, V_DIM), axis=0)
    wb_self = jnp.dot(p_self * coef, expand, preferred_element_type=F32)
    o = acc8[:N_HEADS] + acc8[N_HEADS:] + wb_self[:N_HEADS, :LANES] * vn_ref[...]

    ms = jnp.mean(o * o, axis=1, keepdims=True)
    o_ref[...] = o * lax.rsqrt(ms + NORM_EPS) * subln_ref[...] * (1.0 - lam_init)


def _sample_bias_tables(bias_tab, n_pages):
    past_len = n_pages * PAGE_SIZE
    half_tok = PAGE_SIZE // 2
    rev = bias_tab[:, 1:past_len + 1][:, ::-1]
    vals = rev.reshape(N_HEADS, n_pages, 2, half_tok).transpose(1, 3, 0, 2)
    same_head = jnp.eye(N_HEADS, dtype=bool)
    full = jnp.where(same_head[None, None, :, None, None, :],
                     vals[:, :, :, :, None, None], NEG_INF)
    full = jnp.broadcast_to(full, (n_pages, half_tok, N_HEADS, 2, 2, N_HEADS))
    bias_past = full.reshape(n_pages * half_tok * N_HEADS, 4 * N_HEADS)
    self_vals = jnp.where(same_head[:, None, :], bias_tab[:, 0][:, None, None], NEG_INF)
    self_vals = jnp.broadcast_to(self_vals, (N_HEADS, 2, N_HEADS)).reshape(N_HEADS, 2 * N_HEADS)
    top = jnp.concatenate([self_vals, jnp.full_like(self_vals, NEG_INF)], axis=1)
    bias_self = jnp.concatenate([top, jnp.full_like(top, NEG_INF)], axis=0)
    return bias_past.astype(F32), bias_self.astype(F32)


def _attn_sample(page_table, q, k_new, v_new, cache_k, cache_v, bias_past, bias_self, lamv,
                 subln_row, *, layer, lam_init):
    n_seq, n_pages = page_table.shape
    head_spec = pl.BlockSpec((None, N_HEADS, QK_DIM), lambda s, pt: (s, 0, 0))
    const2 = lambda a: pl.BlockSpec(a.shape, lambda s, pt: (0, 0))

    def page_spec(j):
        return pl.BlockSpec((None, None, PAGE_SIZE * N_HEADS, QK_DIM),
                            lambda s, pt: (layer, pt[s * n_pages + j], 0, 0))

    grid_spec = pltpu.PrefetchScalarGridSpec(
        num_scalar_prefetch=1,
        grid=(n_seq,),
        in_specs=[head_spec, head_spec, head_spec, const2(bias_past), const2(bias_self),
                  const2(lamv), const2(subln_row)]
                 + [page_spec(j) for j in range(n_pages)] * 2,
        out_specs=head_spec,
    )
    return pl.pallas_call(
        functools.partial(_attn_sample_kernel, n_pages=n_pages, lam_init=lam_init),
        grid_spec=grid_spec,
        out_shape=jax.ShapeDtypeStruct(q.shape, F32),
        compiler_params=pltpu.CompilerParams(
            dimension_semantics=("arbitrary",), vmem_limit_bytes=VMEM_LIMIT_BYTES),
        name="attn_sample",
    )(page_table.reshape(-1), q, k_new, v_new, bias_past, bias_self, lamv, subln_row,
      *([cache_k] * n_pages), *([cache_v] * n_pages))


def _pool_prompt_kernel(p_ref, halo_ref, d_ref, ext_ref, *, tp):
    i = pl.program_id(1)
    halo = POOL_STATE + 1
    p = p_ref[...]
    ext_ref[pl.ds(0, halo), :] = jnp.where(i > 0, halo_ref[...], 0.0)
    ext_ref[pl.ds(halo, tp), :] = p
    gd = p.shape[1] // len(POOL_WINDOWS)
    pos1 = i * tp + lax.broadcasted_iota(jnp.int32, (tp, gd), 0) + 1
    for g, w in enumerate(POOL_WINDOWS):
        lanes = pl.ds(g * gd, gd)
        ws = ext_ref[pl.ds(halo, tp), lanes]
        for j in range(1, w):
            ws = ws + ext_ref[pl.ds(halo - j, tp), lanes]
        cnt = jnp.minimum(pos1, w).astype(F32)
        d_ref[:, lanes] = (ws / cnt - p[:, g * gd:(g + 1) * gd]).astype(d_ref.dtype)


def _pool_prompt(p, *, tp):
    b, t, w = p.shape
    halo = POOL_STATE + 1
    per = tp // halo
    return pl.pallas_call(
        functools.partial(_pool_prompt_kernel, tp=tp),
        grid=(b, t // tp),
        in_specs=[pl.BlockSpec((None, tp, w), lambda bi, i: (bi, i, 0)),
                  pl.BlockSpec((None, halo, w), lambda bi, i: (bi, jnp.maximum(i * per - 1, 0), 0))],
        out_specs=pl.BlockSpec((None, tp, w), lambda bi, i: (bi, i, 0)),
        out_shape=jax.ShapeDtypeStruct(p.shape, BF16),
        scratch_shapes=[pltpu.VMEM((tp + halo, w), F32)],
        compiler_params=pltpu.CompilerParams(dimension_semantics=("parallel", "parallel")),
        name="pool_prompt",
    )(p, p)


def _pool_sample_kernel(state_ref, p_ref, d_ref, new_ref, *, past_len):
    w_tot = p_ref.shape[1]
    gd = w_tot // len(POOL_WINDOWS)
    p = p_ref[...]
    for g, w in enumerate(POOL_WINDOWS):
        ws = p[:, g * gd:(g + 1) * gd]
        for j in range(1, w):
            row = POOL_STATE - j
            ws = ws + state_ref[:, pl.ds(row * w_tot + g * gd, gd)]
        cnt = float(min(past_len + 1, w))
        d_ref[:, pl.ds(g * gd, gd)] = (ws / cnt - p[:, g * gd:(g + 1) * gd]).astype(d_ref.dtype)
    keep = (POOL_STATE - 1) * w_tot
    new_ref[:, pl.ds(0, keep)] = state_ref[:, pl.ds(w_tot, keep)]
    new_ref[:, pl.ds(keep, w_tot)] = p


def _pool_sample(state, p, *, past_len):
    return pl.pallas_call(
        functools.partial(_pool_sample_kernel, past_len=past_len),
        out_shape=[jax.ShapeDtypeStruct(p.shape, BF16), jax.ShapeDtypeStruct(state.shape, F32)],
        name="pool_sample",
    )(state, p)


def _merge_out_kernel(h_ref, o_ref, d_ref, mpre_ref, wgate_ref, poolw_ref, pscale_ref, wa_ref,
                      wb_ref, wout_ref, mpost_ref, pre2_ref, post2_ref, wgu_ref, wd_ref, y_ref,
                      *, d_ff):
    h = h_ref[...]
    d_model = h.shape[1]
    u = _rms(h, mpre_ref[...]).astype(BF16)
    gates = jnp.dot(u, wgate_ref[...], preferred_element_type=F32)
    d = d_ref[...]
    gd = d.shape[1] // len(POOL_WINDOWS)
    pooled = jnp.concatenate(
        [jnp.dot(d[:, g * gd:(g + 1) * gd], poolw_ref[g], preferred_element_type=F32)
         for g in range(len(POOL_WINDOWS))], axis=1) * pscale_ref[...]
    branch_a = jnp.dot(o_ref[...], wa_ref[...], preferred_element_type=F32)
    branch_b = jnp.dot(pooled.astype(BF16), wb_ref[...], preferred_element_type=F32)
    merged = (jax.nn.sigmoid(gates[:, :d_model]) * branch_a
              + jax.nn.sigmoid(gates[:, d_model:]) * branch_b)
    mixed = jnp.dot(merged.astype(BF16), wout_ref[...], preferred_element_type=F32)
    h2 = h + _rms(mixed, mpost_ref[...])
    f = _swiglu(_rms(h2, pre2_ref[...]).astype(BF16), wgu_ref, wd_ref, d_ff)
    y_ref[...] = h2 + 0.5 * _rms(f, post2_ref[...])


def _merge_out(h, o, d, weights, *, tm, name):
    b, t, dm = h.shape
    tok = lambda width: pl.BlockSpec((None, tm, width), lambda bi, i: (bi, i, 0))
    d_ff = weights[-1].shape[0]
    return pl.pallas_call(
        functools.partial(_merge_out_kernel, d_ff=d_ff),
        grid=(b, t // tm),
        in_specs=[tok(dm), tok(o.shape[-1]), tok(d.shape[-1])] + [_const_spec(w.shape) for w in weights],
        out_specs=tok(dm),
        out_shape=jax.ShapeDtypeStruct(h.shape, F32),
        compiler_params=pltpu.CompilerParams(
            dimension_semantics=("parallel", "parallel"), vmem_limit_bytes=VMEM_LIMIT_BYTES),
        name=name,
    )(h, o, d, *weights)


TM_PROMPT = 256
TQ = 512
TK = 128
QC = 256
TP = 512


def kernel(x_prompt, x_sample, cache_k, cache_v, state_pool, page_table, rel_bias, ffn1_norm_pre, ffn1_norm_post, ffn1_w_gate, ffn1_w_up, ffn1_w_down, mix_norm_pre, mix_norm_post, w_in, lambda_q1, lambda_k1, lambda_q2, lambda_k2, attn_subln, pool_w, pool_scale, w_branch_a, w_branch_b, w_out, ffn2_norm_pre, ffn2_norm_post, ffn2_w_gate, ffn2_w_up, ffn2_w_down):
    depth = w_in.shape[0]
    bsz, seq, d_model = x_prompt.shape
    n_seq = x_sample.shape[0]
    n_pages = page_table.shape[1]
    past_len = n_pages * PAGE_SIZE
    n_phys = cache_k.shape[1]
    w = ATTN_W
    qkvp_w = 4 * w

    wgu1 = jnp.concatenate([ffn1_w_gate, ffn1_w_up], axis=-1).astype(BF16)
    wgu2 = jnp.concatenate([ffn2_w_gate, ffn2_w_up], axis=-1).astype(BF16)
    wd1 = ffn1_w_down.astype(BF16)
    wd2 = ffn2_w_down.astype(BF16)
    w_qkvp = w_in[:, :, :qkvp_w].astype(BF16)
    w_gate = w_in[:, :, qkvp_w:].astype(BF16)
    wa = w_branch_a.astype(BF16)
    wb = w_branch_b.astype(BF16)
    wo = w_out.astype(BF16)
    pw = pool_w.astype(BF16)
    row = lambda a: a[:, None, :]
    lamv = jnp.stack([lambda_q1, lambda_k1, lambda_q2, lambda_k2], axis=1)

    bias_tab = _bias_by_distance(rel_bias, max(TQ + TK, past_len + 1))
    bias_tiles = _prompt_bias_tiles(bias_tab[:, :TQ + TK], TQ, TK)
    bias_past, bias_self = _sample_bias_tables(bias_tab, n_pages)
    ck = cache_k.reshape(depth, n_phys, PAGE_SIZE * N_HEADS, QK_DIM)
    cv = cache_v.reshape(depth, n_phys, PAGE_SIZE * N_HEADS, V_DIM)

    xp = x_prompt
    xs = x_sample.reshape(n_seq, d_model)
    k_prompt = jnp.zeros((depth, bsz, seq, N_HEADS, QK_DIM), F32)
    v_prompt = jnp.zeros((depth, bsz, seq, N_HEADS, V_DIM), F32)
    outs = [[] for _ in range(4)]
    for l in range(depth):
        lam_init = 0.8 - 0.6 * math.exp(-0.3 * l)
        in_w = (row(ffn1_norm_pre)[l], row(ffn1_norm_post)[l], wgu1[l], wd1[l],
                row(mix_norm_pre)[l], w_qkvp[l])
        out_w = (row(mix_norm_pre)[l], w_gate[l], pw[l], row(pool_scale)[l], wa[l], wb[l], wo[l],
                 row(mix_norm_post)[l], row(ffn2_norm_pre)[l], row(ffn2_norm_post)[l], wgu2[l], wd2[l])

        h, p, qt, kb, vt, k_prompt, v_prompt = _ffn_in_prompt(
            xp, in_w, k_prompt, v_prompt, layer=l, tm=TM_PROMPT, tk=TK)
        o = _attn_prompt(qt, kb, vt, bias_tiles, lamv[l], attn_subln[l][:, None],
                         tq=TQ, tk=TK, qc=QC, lam_init=lam_init)
        d = _pool_prompt(p, tp=TP)
        xp = _merge_out(h, o, d, out_w, tm=TM_PROMPT, name="merge_out_prompt")
        outs[0].append(p[:, seq - POOL_STATE:, :])

        hs, qs, ks, vs, ps = _ffn_in_sample(xs, in_w)
        by_head = lambda a: a.reshape(n_seq, N_HEADS, QK_DIM)
        o_s = _attn_sample(page_table, by_head(qs), by_head(ks), by_head(vs), ck, cv,
                           bias_past, bias_self, lamv[l], attn_subln[l][None],
                           layer=l, lam_init=lam_init)
        d_s, new_state = _pool_sample(state_pool[l].reshape(n_seq, POOL_STATE * w), ps,
                                      past_len=past_len)
        xs = _merge_out(hs[None], o_s.reshape(1, n_seq, w).astype(BF16), d_s[None], out_w,
                        tm=n_seq, name="merge_out_sample")[0]
        outs[1].append(ks.reshape(n_seq, 1, N_HEADS, QK_DIM))
        outs[2].append(vs.reshape(n_seq, 1, N_HEADS, V_DIM))
        outs[3].append(new_state.reshape(n_seq, POOL_STATE, w))

    stacked = [jnp.stack(o) for o in outs]
    return (xp, xs.reshape(n_seq, 1, d_model), k_prompt, v_prompt, stacked[0],
            stacked[1], stacked[2], stacked[3])
```

```python
import functools
import math

import jax
import jax.numpy as jnp
from jax import lax
from jax.experimental import pallas as pl
from jax.experimental.pallas import tpu as pltpu

F32 = jnp.float32
BF16 = jnp.bfloat16

N_HEADS = 4
HEAD_DIM = 64
QK_DIM = 2 * HEAD_DIM
V_DIM = 2 * HEAD_DIM
ATTN_W = N_HEADS * V_DIM
POOL_WINDOWS = (2, 4, 8, 16)
POOL_STATE = max(POOL_WINDOWS) - 1
N_BUCKETS = 32
MAX_DISTANCE = 128
PAGE_SIZE = 128
NORM_EPS = 1e-6
NEG_INF = -1e30
SCALE = HEAD_DIM ** -0.5
LOG2E = math.log2(math.e)

LANES = 128
SUBLANES = 8
VMEM_LIMIT_BYTES = 56 * 1024 * 1024


def _rms(x, g):
    ms = jnp.mean(x * x, axis=-1, keepdims=True)
    return x * lax.rsqrt(ms + NORM_EPS) * g


def _const_spec(shape):
    nd = len(shape)
    return pl.BlockSpec(shape, lambda *_: (0,) * nd, pipeline_mode=pl.Buffered(1))


def _swiglu(u, wgu_ref, wd_ref, d_ff):
    gu = jnp.dot(u, wgu_ref[...], preferred_element_type=F32)
    g = gu[:, :d_ff]
    act = (g * jax.nn.sigmoid(g) * gu[:, d_ff:]).astype(BF16)
    return jnp.dot(act, wd_ref[...], preferred_element_type=F32)


def _ffn_in_body(x_ref, pre_ref, post_ref, wgu_ref, wd_ref, mpre_ref, win_ref, d_ff):
    x = x_ref[...]
    f = _swiglu(_rms(x, pre_ref[...]).astype(BF16), wgu_ref, wd_ref, d_ff)
    h = x + 0.5 * _rms(f, post_ref[...])
    u = _rms(h, mpre_ref[...]).astype(BF16)
    proj = jnp.dot(u, win_ref[...], preferred_element_type=F32)
    w = ATTN_W
    return h, proj[:, :w], proj[:, w:2 * w], proj[:, 2 * w:3 * w], proj[:, 3 * w:]


def _ffn_in_prompt_kernel(x_ref, pre_ref, post_ref, wgu_ref, wd_ref, mpre_ref, win_ref,
                          kbuf_ref, vbuf_ref, h_ref, p_ref, qb_ref, kb_ref, vb_ref, k5_ref, v5_ref,
                          *, d_ff):
    del kbuf_ref, vbuf_ref
    h, q, k, v, p = _ffn_in_body(x_ref, pre_ref, post_ref, wgu_ref, wd_ref, mpre_ref, win_ref, d_ff)
    h_ref[...] = h
    p_ref[...] = p
    qb_ref[...] = (q * (SCALE * LOG2E)).astype(BF16)
    kb_ref[...] = k.astype(BF16)
    vb_ref[...] = v.astype(BF16)
    for hd in range(N_HEADS):
        k5_ref[:, hd, :] = k[:, hd * QK_DIM:(hd + 1) * QK_DIM]
        v5_ref[:, hd, :] = v[:, hd * V_DIM:(hd + 1) * V_DIM]


def _ffn_in_prompt(x, weights, kbuf, vbuf, *, layer, tm):
    b, t, d = x.shape
    d_ff = weights[3].shape[0]
    w = ATTN_W
    tok = lambda width: pl.BlockSpec((None, tm, width), lambda bi, i: (bi, i, 0))
    kv5 = pl.BlockSpec((None, None, tm, N_HEADS, QK_DIM), lambda bi, i: (layer, bi, i, 0, 0))
    any_spec = pl.BlockSpec(memory_space=pl.ANY)
    n_in = 1 + len(weights)
    return pl.pallas_call(
        functools.partial(_ffn_in_prompt_kernel, d_ff=d_ff),
        grid=(b, t // tm),
        in_specs=[tok(d)] + [_const_spec(a.shape) for a in weights] + [any_spec, any_spec],
        out_specs=[tok(d), tok(w), tok(w), tok(w), tok(w), kv5, kv5],
        out_shape=[jax.ShapeDtypeStruct((b, t, d), F32),
                   jax.ShapeDtypeStruct((b, t, w), F32),
                   jax.ShapeDtypeStruct((b, t, w), BF16),
                   jax.ShapeDtypeStruct((b, t, w), BF16),
                   jax.ShapeDtypeStruct((b, t, w), BF16),
                   jax.ShapeDtypeStruct(kbuf.shape, F32),
                   jax.ShapeDtypeStruct(vbuf.shape, F32)],
        input_output_aliases={n_in: 5, n_in + 1: 6},
        compiler_params=pltpu.CompilerParams(
            dimension_semantics=("parallel", "parallel"), vmem_limit_bytes=VMEM_LIMIT_BYTES),
        name="ffn_in_prompt",
    )(x, *weights, kbuf, vbuf)


def _ffn_in_sample_kernel(x_ref, pre_ref, post_ref, wgu_ref, wd_ref, mpre_ref, win_ref,
                          h_ref, q_ref, k_ref, v_ref, p_ref, *, d_ff):
    h, q, k, v, p = _ffn_in_body(x_ref, pre_ref, post_ref, wgu_ref, wd_ref, mpre_ref, win_ref, d_ff)
    h_ref[...] = h
    q_ref[...] = q * SCALE
    k_ref[...] = k
    v_ref[...] = v
    p_ref[...] = p


def _ffn_in_sample(x, weights):
    n, d = x.shape
    d_ff = weights[3].shape[0]
    return pl.pallas_call(
        functools.partial(_ffn_in_sample_kernel, d_ff=d_ff),
        out_shape=[jax.ShapeDtypeStruct((n, d), F32)] + [jax.ShapeDtypeStruct((n, ATTN_W), F32)] * 4,
        compiler_params=pltpu.CompilerParams(vmem_limit_bytes=VMEM_LIMIT_BYTES),
        name="ffn_in_sample",
    )(x, *weights)


def _lambda_value(lamv, lam_init):
    s1 = jnp.sum(lamv[0:1] * lamv[1:2], axis=1, keepdims=True)
    s2 = jnp.sum(lamv[2:3] * lamv[3:4], axis=1, keepdims=True)
    return jnp.exp(s1) - jnp.exp(s2) + lam_init


def _bias_by_distance(rel_bias, n):
    dist = jnp.arange(n, dtype=jnp.int32)
    max_exact = N_BUCKETS // 2
    nf = jnp.maximum(dist, 1).astype(F32)
    large = max_exact + (jnp.log(nf / max_exact) / math.log(MAX_DISTANCE / max_exact)
                         * (N_BUCKETS - max_exact)).astype(jnp.int32)
    large = jnp.minimum(large, N_BUCKETS - 1)
    bucket = jnp.where(dist < max_exact, dist, large)
    return rel_bias[bucket].T.astype(F32)


def _attn_prompt_kernel(q_ref, k_ref, v_ref, bias_ref, lamv_ref, subln_ref, o_ref,
                        vt_ref, q2_ref, m_ref, l_ref, acc_ref, *, tq, tk, qc, lam_init):
    qi = pl.program_id(2)
    ratio = tq // tk
    n_special = bias_ref.shape[0]
    n_chunks = 2 * tq // qc

    @pl.when(qi == 0)
    def _():
        def transpose_tile(j, carry):
            start = pl.multiple_of(j * tk, tk)
            vt_ref[j] = v_ref[pl.ds(start, tk), :].T
            return carry
        lax.fori_loop(0, vt_ref.shape[0], transpose_tile, 0)

    qt = q_ref[...].T
    row = lax.broadcasted_iota(jnp.int32, qt.shape, 0)
    zero = jnp.zeros_like(qt)
    q2_ref[:, pl.ds(0, tq)] = jnp.where(row < HEAD_DIM, qt, zero)
    q2_ref[:, pl.ds(tq, tq)] = jnp.where(row >= HEAD_DIM, qt, zero)
    m_ref[...] = jnp.full(m_ref.shape, NEG_INF, F32)
    l_ref[...] = jnp.zeros(l_ref.shape, F32)
    acc_ref[...] = jnp.zeros(acc_ref.shape, F32)

    def step(kj, special):
        start = pl.multiple_of(kj * tk, tk)
        k = k_ref[pl.ds(start, tk), :]
        vt = vt_ref[kj]
        for c in range(n_chunks):
            q_lo = (c * qc) % tq
            if special is not None and q_lo + qc - 1 < (special - 1) * tk:
                continue
            cols = pl.ds(c * qc, qc)
            s = jnp.dot(k, q2_ref[:, cols], preferred_element_type=F32)
            if special is not None:
                s = s + bias_ref[special, :, pl.ds(q_lo, qc)]
            m_old = m_ref[:, cols]
            m_new = jnp.maximum(m_old, jnp.max(s, axis=0, keepdims=True))
            alpha = jnp.exp2(m_old - m_new)
            p = jnp.exp2(s - m_new)
            l_ref[:, cols] = alpha * l_ref[:, cols] + jnp.sum(p, axis=0, keepdims=True)
            pv = jnp.dot(vt, p.astype(BF16), preferred_element_type=F32)
            acc_ref[:, cols] = alpha * acc_ref[:, cols] + pv
            m_ref[:, cols] = m_new

    def run_tiles(first, specials):
        for t, special in enumerate(specials):
            step(first + t, special)

    n_far_blocks = jnp.maximum(qi - 1, 0)

    def far_group(g, carry):
        run_tiles(g * 2 * ratio, [None] * (2 * ratio))
        return carry

    lax.fori_loop(0, n_far_blocks // 2, far_group, 0)

    @pl.when(n_far_blocks % 2 == 1)
    def _():
        run_tiles((n_far_blocks - 1) * ratio, [None] * ratio)

    @pl.when(qi > 0)
    def _():
        run_tiles((qi - 1) * ratio, [None] * (ratio - 1) + list(range(n_special)))

    @pl.when(qi == 0)
    def _():
        run_tiles(0, list(range(1, n_special)))

    l = l_ref[...]
    acc = acc_ref[...]
    lam = _lambda_value(lamv_ref[...], lam_init)
    o = acc[:, :tq] / l[:, :tq] - lam * (acc[:, tq:] / l[:, tq:])
    ms = jnp.mean(o * o, axis=0, keepdims=True)
    y = o * lax.rsqrt(ms + NORM_EPS) * subln_ref[...] * (1.0 - lam_init)
    o_ref[...] = y.T.astype(BF16)


def _prompt_bias_tiles(bias_tab, tq, tk):
    n_special = tq // tk + 1
    rows = n_special * tk
    period = rows + tq
    x = jnp.arange(period, dtype=jnp.int32)
    dist = jnp.where(x < tq, x, x - period) + tk
    shifted = (bias_tab - bias_tab[:, -1:]) * LOG2E
    vals = jnp.where(dist < 0, NEG_INF, shifted[:, jnp.clip(dist, 0, bias_tab.shape[1] - 1)])
    flat = jnp.tile(vals, (1, rows))[:, :rows * (period - 1)]
    toeplitz = flat.reshape(-1, rows, period - 1)[:, :, :tq]
    return toeplitz.reshape(-1, n_special, tk, tq).astype(F32)


def _attn_prompt(qb, kb, vb, bias_tiles, lamv, subln_col, *, tq, tk, qc, lam_init):
    b, t, w = qb.shape
    nk = t // tk
    n_special = bias_tiles.shape[1]
    grid = (b, N_HEADS, t // tq)
    return pl.pallas_call(
        functools.partial(_attn_prompt_kernel, tq=tq, tk=tk, qc=qc, lam_init=lam_init),
        grid=grid,
        in_specs=[
            pl.BlockSpec((None, tq, QK_DIM), lambda bi, h, qi: (bi, qi, h)),
            pl.BlockSpec((None, t, QK_DIM), lambda bi, h, qi: (bi, 0, h)),
            pl.BlockSpec((None, t, V_DIM), lambda bi, h, qi: (bi, 0, h)),
            pl.BlockSpec((None, n_special, tk, tq), lambda bi, h, qi: (h, 0, 0, 0)),
            pl.BlockSpec(lamv.shape, lambda bi, h, qi: (0, 0)),
            pl.BlockSpec(subln_col.shape, lambda bi, h, qi: (0, 0)),
        ],
        out_specs=pl.BlockSpec((None, tq, V_DIM), lambda bi, h, qi: (bi, qi, h)),
        out_shape=jax.ShapeDtypeStruct((b, t, w), BF16),
        scratch_shapes=[pltpu.VMEM((nk, V_DIM, tk), BF16),
                        pltpu.VMEM((QK_DIM, 2 * tq), BF16),
                        pltpu.VMEM((1, 2 * tq), F32),
                        pltpu.VMEM((1, 2 * tq), F32),
                        pltpu.VMEM((V_DIM, 2 * tq), F32)],
        compiler_params=pltpu.CompilerParams(
            dimension_semantics=("parallel", "parallel", "arbitrary"),
            vmem_limit_bytes=VMEM_LIMIT_BYTES),
        name="attn_prompt",
    )(qb, kb, vb, bias_tiles, lamv, subln_col)


def _attn_sample_kernel(pt_ref, q_ref, kn_ref, vn_ref, bias_ref, bias_self_ref, lamv_ref,
                        subln_ref, *rest, n_pages, seqs, lam_init):
    del pt_ref
    o_ref = rest[2 * seqs * n_pages]
    for i in range(seqs):
        kp_refs = rest[i * n_pages:(i + 1) * n_pages]
        vp_refs = rest[(seqs + i) * n_pages:(seqs + i + 1) * n_pages]
        o_ref[i] = _attn_sample_one(q_ref[i], kn_ref[i], vn_ref[i], bias_ref, bias_self_ref,
                                    lamv_ref, subln_ref, kp_refs, vp_refs, lam_init)


def _attn_sample_one(qh, kn, vn, bias_ref, bias_self_ref, lamv_ref, subln_ref, kp_refs, vp_refs,
                     lam_init):
    nt = (((1,), (1,)), ((), ()))
    half = PAGE_SIZE * N_HEADS // 2
    n_col = 2 * N_HEADS

    lane = lax.broadcasted_iota(jnp.int32, qh.shape, 1)
    zero = jnp.zeros_like(qh)
    q8 = jnp.concatenate([jnp.where(lane < HEAD_DIM, qh, zero),
                          jnp.where(lane >= HEAD_DIM, qh, zero)], axis=0)
    z8 = jnp.zeros_like(q8)
    rt = jnp.concatenate([jnp.concatenate([q8, z8], axis=1),
                          jnp.concatenate([z8, q8], axis=1)], axis=0)

    def paired(ref):
        return jnp.concatenate([ref[pl.ds(0, half), :], ref[pl.ds(half, half), :]], axis=1)

    s = jnp.concatenate([lax.dot_general(paired(kp), rt, nt, preferred_element_type=F32)
                         for kp in kp_refs], axis=0) + bias_ref[...]
    kn8 = jnp.concatenate([kn, kn], axis=0)
    s_self = lax.dot_general(jnp.concatenate([kn8, jnp.zeros_like(kn8)], axis=1), rt, nt,
                             preferred_element_type=F32) + bias_self_ref[...]

    m16 = jnp.maximum(jnp.max(s, axis=0, keepdims=True), jnp.max(s_self, axis=0, keepdims=True))
    m8 = jnp.maximum(m16[:, :n_col], m16[:, n_col:])
    m = jnp.concatenate([m8, m8], axis=1)
    p = jnp.exp(s - m)
    p_self = jnp.exp(s_self - m)
    l16 = jnp.sum(p, axis=0, keepdims=True) + jnp.sum(p_self, axis=0, keepdims=True)
    l8 = l16[:, :n_col] + l16[:, n_col:]
    lam = _lambda_value(lamv_ref[...], lam_init)
    col = lax.broadcasted_iota(jnp.int32, l8.shape, 1)
    coef8 = jnp.where(col < N_HEADS, 1.0, -lam) / l8
    coef = jnp.concatenate([coef8, coef8], axis=1)

    erow = lax.broadcasted_iota(jnp.int32, (2 * n_col, 2 * LANES), 0)
    elane = lax.broadcasted_iota(jnp.int32, (2 * n_col, 2 * LANES), 1)
    expand = ((erow < n_col) == (elane < LANES)).astype(F32)

    acc = jnp.zeros((half, V_DIM), F32)
    for j, vp in enumerate(vp_refs):
        wb = jnp.dot(p[j * half:(j + 1) * half] * coef, expand, preferred_element_type=F32)
        acc = acc + wb[:, :LANES] * vp[pl.ds(0, half), :] + wb[:, LANES:] * vp[pl.ds(half, half), :]
    acc8 = jnp.sum(acc.reshape(half // SUBLANES, SUBLANES, V_DIM), axis=0)
    wb_self = jnp.dot(p_self * coef, expand, preferred_element_type=F32)
    o = acc8[:N_HEADS] + acc8[N_HEADS:] + wb_self[:N_HEADS, :LANES] * vn

    ms = jnp.mean(o * o, axis=1, keepdims=True)
    return o * lax.rsqrt(ms + NORM_EPS) * subln_ref[...] * (1.0 - lam_init)


def _sample_bias_tables(bias_tab, n_pages):
    past_len = n_pages * PAGE_SIZE
    half_tok = PAGE_SIZE // 2
    rev = bias_tab[:, 1:past_len + 1][:, ::-1]
    vals = rev.reshape(N_HEADS, n_pages, 2, half_tok).transpose(1, 3, 0, 2)
    same_head = jnp.eye(N_HEADS, dtype=bool)
    full = jnp.where(same_head[None, None, :, None, None, :],
                     vals[:, :, :, :, None, None], NEG_INF)
    full = jnp.broadcast_to(full, (n_pages, half_tok, N_HEADS, 2, 2, N_HEADS))
    bias_past = full.reshape(n_pages * half_tok * N_HEADS, 4 * N_HEADS)
    self_vals = jnp.where(same_head[:, None, :], bias_tab[:, 0][:, None, None], NEG_INF)
    self_vals = jnp.broadcast_to(self_vals, (N_HEADS, 2, N_HEADS)).reshape(N_HEADS, 2 * N_HEADS)
    top = jnp.concatenate([self_vals, jnp.full_like(self_vals, NEG_INF)], axis=1)
    bias_self = jnp.concatenate([top, jnp.full_like(top, NEG_INF)], axis=0)
    return bias_past.astype(F32), bias_self.astype(F32)


def _attn_sample(page_table, q, k_new, v_new, cache_k, cache_v, bias_past, bias_self, lamv,
                 subln_row, *, layer, lam_init):
    n_seq, n_pages = page_table.shape
    seqs = SEQS_PER_STEP
    head_spec = pl.BlockSpec((seqs, N_HEADS, QK_DIM), lambda s, pt: (s, 0, 0))
    const2 = lambda a: pl.BlockSpec(a.shape, lambda s, pt: (0, 0))

    def page_spec(i, j):
        return pl.BlockSpec((None, None, PAGE_SIZE * N_HEADS, QK_DIM),
                            lambda s, pt: (layer, pt[(s * seqs + i) * n_pages + j], 0, 0))

    page_specs = [page_spec(i, j) for i in range(seqs) for j in range(n_pages)]
    grid_spec = pltpu.PrefetchScalarGridSpec(
        num_scalar_prefetch=1,
        grid=(n_seq // seqs,),
        in_specs=[head_spec, head_spec, head_spec, const2(bias_past), const2(bias_self),
                  const2(lamv), const2(subln_row)] + page_specs * 2,
        out_specs=head_spec,
    )
    n_page_args = seqs * n_pages
    return pl.pallas_call(
        functools.partial(_attn_sample_kernel, n_pages=n_pages, seqs=seqs, lam_init=lam_init),
        grid_spec=grid_spec,
        out_shape=jax.ShapeDtypeStruct(q.shape, F32),
        compiler_params=pltpu.CompilerParams(
            dimension_semantics=("arbitrary",), vmem_limit_bytes=VMEM_LIMIT_BYTES),
        name="attn_sample",
    )(page_table.reshape(-1), q, k_new, v_new, bias_past, bias_self, lamv, subln_row,
      *([cache_k] * n_page_args), *([cache_v] * n_page_args))


def _pool_prompt_kernel(p_ref, halo_ref, d_ref, ext_ref, *, tp):
    i = pl.program_id(1)
    halo = POOL_STATE + 1
    p = p_ref[...]
    ext_ref[pl.ds(0, halo), :] = jnp.where(i > 0, halo_ref[...], 0.0)
    ext_ref[pl.ds(halo, tp), :] = p
    gd = p.shape[1] // len(POOL_WINDOWS)
    pos1 = i * tp + lax.broadcasted_iota(jnp.int32, (tp, gd), 0) + 1
    for g, w in enumerate(POOL_WINDOWS):
        lanes = pl.ds(g * gd, gd)
        ws = ext_ref[pl.ds(halo, tp), lanes]
        for j in range(1, w):
            ws = ws + ext_ref[pl.ds(halo - j, tp), lanes]
        cnt = jnp.minimum(pos1, w).astype(F32)
        d_ref[:, lanes] = (ws / cnt - p[:, g * gd:(g + 1) * gd]).astype(d_ref.dtype)


def _pool_prompt(p, *, tp):
    b, t, w = p.shape
    halo = POOL_STATE + 1
    per = tp // halo
    return pl.pallas_call(
        functools.partial(_pool_prompt_kernel, tp=tp),
        grid=(b, t // tp),
        in_specs=[pl.BlockSpec((None, tp, w), lambda bi, i: (bi, i, 0)),
                  pl.BlockSpec((None, halo, w), lambda bi, i: (bi, jnp.maximum(i * per - 1, 0), 0))],
        out_specs=pl.BlockSpec((None, tp, w), lambda bi, i: (bi, i, 0)),
        out_shape=jax.ShapeDtypeStruct(p.shape, BF16),
        scratch_shapes=[pltpu.VMEM((tp + halo, w), F32)],
        compiler_params=pltpu.CompilerParams(dimension_semantics=("parallel", "parallel")),
        name="pool_prompt",
    )(p, p)


def _pool_sample_kernel(state_ref, p_ref, d_ref, new_ref, *, past_len):
    w_tot = p_ref.shape[1]
    gd = w_tot // len(POOL_WINDOWS)
    p = p_ref[...]
    for g, w in enumerate(POOL_WINDOWS):
        ws = p[:, g * gd:(g + 1) * gd]
        for j in range(1, w):
            row = POOL_STATE - j
            ws = ws + state_ref[:, pl.ds(row * w_tot + g * gd, gd)]
        cnt = float(min(past_len + 1, w))
        d_ref[:, pl.ds(g * gd, gd)] = (ws / cnt - p[:, g * gd:(g + 1) * gd]).astype(d_ref.dtype)
    keep = (POOL_STATE - 1) * w_tot
    new_ref[:, pl.ds(0, keep)] = state_ref[:, pl.ds(w_tot, keep)]
    new_ref[:, pl.ds(keep, w_tot)] = p


def _pool_sample(state, p, *, past_len):
    return pl.pallas_call(
        functools.partial(_pool_sample_kernel, past_len=past_len),
        out_shape=[jax.ShapeDtypeStruct(p.shape, BF16), jax.ShapeDtypeStruct(state.shape, F32)],
        name="pool_sample",
    )(state, p)


def _merge_out_kernel(h_ref, o_ref, d_ref, mpre_ref, wgate_ref, poolw_ref, pscale_ref, wa_ref,
                      wb_ref, wout_ref, mpost_ref, pre2_ref, post2_ref, wgu_ref, wd_ref, y_ref,
                      *, d_ff):
    h = h_ref[...]
    d_model = h.shape[1]
    u = _rms(h, mpre_ref[...]).astype(BF16)
    gates = jnp.dot(u, wgate_ref[...], preferred_element_type=F32)
    d = d_ref[...]
    gd = d.shape[1] // len(POOL_WINDOWS)
    pooled = jnp.concatenate(
        [jnp.dot(d[:, g * gd:(g + 1) * gd], poolw_ref[g], preferred_element_type=F32)
         for g in range(len(POOL_WINDOWS))], axis=1) * pscale_ref[...]
    branch_a = jnp.dot(o_ref[...], wa_ref[...], preferred_element_type=F32)
    branch_b = jnp.dot(pooled.astype(BF16), wb_ref[...], preferred_element_type=F32)
    merged = (jax.nn.sigmoid(gates[:, :d_model]) * branch_a
              + jax.nn.sigmoid(gates[:, d_model:]) * branch_b)
    mixed = jnp.dot(merged.astype(BF16), wout_ref[...], preferred_element_type=F32)
    h2 = h + _rms(mixed, mpost_ref[...])
    f = _swiglu(_rms(h2, pre2_ref[...]).astype(BF16), wgu_ref, wd_ref, d_ff)
    y_ref[...] = h2 + 0.5 * _rms(f, post2_ref[...])


def _merge_out(h, o, d, weights, *, tm, name):
    b, t, dm = h.shape
    tok = lambda width: pl.BlockSpec((None, tm, width), lambda bi, i: (bi, i, 0))
    d_ff = weights[-1].shape[0]
    return pl.pallas_call(
        functools.partial(_merge_out_kernel, d_ff=d_ff),
        grid=(b, t // tm),
        in_specs=[tok(dm), tok(o.shape[-1]), tok(d.shape[-1])] + [_const_spec(w.shape) for w in weights],
        out_specs=tok(dm),
        out_shape=jax.ShapeDtypeStruct(h.shape, F32),
        compiler_params=pltpu.CompilerParams(
            dimension_semantics=("parallel", "parallel"), vmem_limit_bytes=VMEM_LIMIT_BYTES),
        name=name,
    )(h, o, d, *weights)


TM_PROMPT = 256
TQ = 512
TK = 128
QC = 256
TP = 512
SEQS_PER_STEP = 2


def kernel(x_prompt, x_sample, cache_k, cache_v, state_pool, page_table, rel_bias, ffn1_norm_pre, ffn1_norm_post, ffn1_w_gate, ffn1_w_up, ffn1_w_down, mix_norm_pre, mix_norm_post, w_in, lambda_q1, lambda_k1, lambda_q2, lambda_k2, attn_subln, pool_w, pool_scale, w_branch_a, w_branch_b, w_out, ffn2_norm_pre, ffn2_norm_post, ffn2_w_gate, ffn2_w_up, ffn2_w_down):
    depth = w_in.shape[0]
    bsz, seq, d_model = x_prompt.shape
    n_seq = x_sample.shape[0]
    n_pages = page_table.shape[1]
    past_len = n_pages * PAGE_SIZE
    n_phys = cache_k.shape[1]
    w = ATTN_W
    qkvp_w = 4 * w

    wgu1 = jnp.concatenate([ffn1_w_gate, ffn1_w_up], axis=-1).astype(BF16)
    wgu2 = jnp.concatenate([ffn2_w_gate, ffn2_w_up], axis=-1).astype(BF16)
    wd1 = ffn1_w_down.astype(BF16)
    wd2 = ffn2_w_down.astype(BF16)
    w_qkvp = w_in[:, :, :qkvp_w].astype(BF16)
    w_gate = w_in[:, :, qkvp_w:].astype(BF16)
    wa = w_branch_a.astype(BF16)
    wb = w_branch_b.astype(BF16)
    wo = w_out.astype(BF16)
    pw = pool_w.astype(BF16)
    row = lambda a: a[:, None, :]
    lamv = jnp.stack([lambda_q1, lambda_k1, lambda_q2, lambda_k2], axis=1)

    bias_tab = _bias_by_distance(rel_bias, max(TQ + TK, past_len + 1))
    bias_tiles = _prompt_bias_tiles(bias_tab[:, :TQ + TK], TQ, TK)
    bias_past, bias_self = _sample_bias_tables(bias_tab, n_pages)
    ck = cache_k.reshape(depth, n_phys, PAGE_SIZE * N_HEADS, QK_DIM)
    cv = cache_v.reshape(depth, n_phys, PAGE_SIZE * N_HEADS, V_DIM)

    xp = x_prompt
    xs = x_sample.reshape(n_seq, d_model)
    k_prompt = jnp.zeros((depth, bsz, seq, N_HEADS, QK_DIM), F32)
    v_prompt = jnp.zeros((depth, bsz, seq, N_HEADS, V_DIM), F32)
    outs = [[] for _ in range(4)]
    for l in range(depth):
        lam_init = 0.8 - 0.6 * math.exp(-0.3 * l)
        in_w = (row(ffn1_norm_pre)[l], row(ffn1_norm_post)[l], wgu1[l], wd1[l],
                row(mix_norm_pre)[l], w_qkvp[l])
        out_w = (row(mix_norm_pre)[l], w_gate[l], pw[l], row(pool_scale)[l], wa[l], wb[l], wo[l],
                 row(mix_norm_post)[l], row(ffn2_norm_pre)[l], row(ffn2_norm_post)[l], wgu2[l], wd2[l])

        h, p, qb, kb, vb, k_prompt, v_prompt = _ffn_in_prompt(
            xp, in_w, k_prompt, v_prompt, layer=l, tm=TM_PROMPT)
        o = _attn_prompt(qb, kb, vb, bias_tiles, lamv[l], attn_subln[l][:, None],
                         tq=TQ, tk=TK, qc=QC, lam_init=lam_init)
        d = _pool_prompt(p, tp=TP)
        xp = _merge_out(h, o, d, out_w, tm=TM_PROMPT, name="merge_out_prompt")
        outs[0].append(p[:, seq - POOL_STATE:, :])

        hs, qs, ks, vs, ps = _ffn_in_sample(xs, in_w)
        by_head = lambda a: a.reshape(n_seq, N_HEADS, QK_DIM)
        o_s = _attn_sample(page_table, by_head(qs), by_head(ks), by_head(vs), ck, cv,
                           bias_past, bias_self, lamv[l], attn_subln[l][None],
                           layer=l, lam_init=lam_init)
        d_s, new_state = _pool_sample(state_pool[l].reshape(n_seq, POOL_STATE * w), ps,
                                      past_len=past_len)
        xs = _merge_out(hs[None], o_s.reshape(1, n_seq, w).astype(BF16), d_s[None], out_w,
                        tm=n_seq, name="merge_out_sample")[0]
        outs[1].append(ks.reshape(n_seq, 1, N_HEADS, QK_DIM))
        outs[2].append(vs.reshape(n_seq, 1, N_HEADS, V_DIM))
        outs[3].append(new_state.reshape(n_seq, POOL_STATE, w))

    stacked = [jnp.stack(o) for o in outs]
    return (xp, xs.reshape(n_seq, 1, d_model), k_prompt, v_prompt, stacked[0],
            stacked[1], stacked[2], stacked[3])
```

```python
import functools
import math

import jax
import jax.numpy as jnp
from jax import lax
from jax.experimental import pallas as pl
from jax.experimental.pallas import tpu as pltpu

F32 = jnp.float32
BF16 = jnp.bfloat16

N_HEADS = 4
HEAD_DIM = 64
QK_DIM = 2 * HEAD_DIM
V_DIM = 2 * HEAD_DIM
ATTN_W = N_HEADS * V_DIM
POOL_WINDOWS = (2, 4, 8, 16)
POOL_STATE = max(POOL_WINDOWS) - 1
N_BUCKETS = 32
MAX_DISTANCE = 128
PAGE_SIZE = 128
NORM_EPS = 1e-6
NEG_INF = -1e30
SCALE = HEAD_DIM ** -0.5
LOG2E = math.log2(math.e)

LANES = 128
SUBLANES = 8
VMEM_LIMIT_BYTES = 56 * 1024 * 1024


def _rms(x, g):
    ms = jnp.mean(x * x, axis=-1, keepdims=True)
    return x * lax.rsqrt(ms + NORM_EPS) * g


def _layer_spec(stacked, layer):
    zeros = (0,) * (stacked.ndim - 1)
    return pl.BlockSpec((None,) + stacked.shape[1:], lambda *_: (layer,) + zeros,
                        pipeline_mode=pl.Buffered(1))


def _swiglu(u, wgu_ref, wd_ref, d_ff):
    gu = jnp.dot(u, wgu_ref[...], preferred_element_type=F32)
    g = gu[:, :d_ff]
    act = (g * jax.nn.sigmoid(g) * gu[:, d_ff:]).astype(BF16)
    return jnp.dot(act, wd_ref[...], preferred_element_type=F32)


def _ffn_in_body(x_ref, pre_ref, post_ref, wgu_ref, wd_ref, mpre_ref, win_ref, d_ff):
    x = x_ref[...]
    f = _swiglu(_rms(x, pre_ref[...]).astype(BF16), wgu_ref, wd_ref, d_ff)
    h = x + 0.5 * _rms(f, post_ref[...])
    u = _rms(h, mpre_ref[...]).astype(BF16)
    proj = jnp.dot(u, win_ref[...], preferred_element_type=F32)
    w = ATTN_W
    return h, proj[:, :w], proj[:, w:2 * w], proj[:, 2 * w:3 * w], proj[:, 3 * w:]


def _ffn_in_prompt_kernel(x_ref, pre_ref, post_ref, wgu_ref, wd_ref, mpre_ref, win_ref,
                          kbuf_ref, vbuf_ref, h_ref, p_ref, qb_ref, kb_ref, vb_ref, k5_ref, v5_ref,
                          *, d_ff):
    del kbuf_ref, vbuf_ref
    h, q, k, v, p = _ffn_in_body(x_ref, pre_ref, post_ref, wgu_ref, wd_ref, mpre_ref, win_ref, d_ff)
    h_ref[...] = h
    p_ref[...] = p
    qb_ref[...] = (q * (SCALE * LOG2E)).astype(BF16)
    kb_ref[...] = k.astype(BF16)
    vb_ref[...] = v.astype(BF16)
    for hd in range(N_HEADS):
        k5_ref[:, hd, :] = k[:, hd * QK_DIM:(hd + 1) * QK_DIM]
        v5_ref[:, hd, :] = v[:, hd * V_DIM:(hd + 1) * V_DIM]


def _ffn_in_prompt(x, weights, kbuf, vbuf, *, layer, tm):
    b, t, d = x.shape
    d_ff = weights[3].shape[1]
    w = ATTN_W
    tok = lambda width: pl.BlockSpec((None, tm, width), lambda bi, i: (bi, i, 0))
    kv5 = pl.BlockSpec((None, None, tm, N_HEADS, QK_DIM), lambda bi, i: (layer, bi, i, 0, 0))
    any_spec = pl.BlockSpec(memory_space=pl.ANY)
    n_in = 1 + len(weights)
    return pl.pallas_call(
        functools.partial(_ffn_in_prompt_kernel, d_ff=d_ff),
        grid=(b, t // tm),
        in_specs=[tok(d)] + [_layer_spec(a, layer) for a in weights] + [any_spec, any_spec],
        out_specs=[tok(d), tok(w), tok(w), tok(w), tok(w), kv5, kv5],
        out_shape=[jax.ShapeDtypeStruct((b, t, d), F32),
                   jax.ShapeDtypeStruct((b, t, w), F32),
                   jax.ShapeDtypeStruct((b, t, w), BF16),
                   jax.ShapeDtypeStruct((b, t, w), BF16),
                   jax.ShapeDtypeStruct((b, t, w), BF16),
                   jax.ShapeDtypeStruct(kbuf.shape, F32),
                   jax.ShapeDtypeStruct(vbuf.shape, F32)],
        input_output_aliases={n_in: 5, n_in + 1: 6},
        compiler_params=pltpu.CompilerParams(
            dimension_semantics=("parallel", "parallel"), vmem_limit_bytes=VMEM_LIMIT_BYTES),
        name="ffn_in_prompt",
    )(x, *weights, kbuf, vbuf)


def _ffn_in_sample_kernel(x_ref, pre_ref, post_ref, wgu_ref, wd_ref, mpre_ref, win_ref,
                          h_ref, q_ref, k_ref, v_ref, p_ref, *, d_ff):
    h, q, k, v, p = _ffn_in_body(x_ref, pre_ref, post_ref, wgu_ref, wd_ref, mpre_ref, win_ref, d_ff)
    h_ref[...] = h
    q_ref[...] = q * SCALE
    k_ref[...] = k
    v_ref[...] = v
    p_ref[...] = p


def _ffn_in_sample(x, weights, *, layer):
    n, d = x.shape
    d_ff = weights[3].shape[1]
    whole = lambda width: pl.BlockSpec((n, width), lambda i: (0, 0))
    return pl.pallas_call(
        functools.partial(_ffn_in_sample_kernel, d_ff=d_ff),
        grid=(1,),
        in_specs=[whole(d)] + [_layer_spec(a, layer) for a in weights],
        out_specs=[whole(d)] + [whole(ATTN_W)] * 4,
        out_shape=[jax.ShapeDtypeStruct((n, d), F32)] + [jax.ShapeDtypeStruct((n, ATTN_W), F32)] * 4,
        compiler_params=pltpu.CompilerParams(
            dimension_semantics=("arbitrary",), vmem_limit_bytes=VMEM_LIMIT_BYTES),
        name="ffn_in_sample",
    )(x, *weights)


def _lambda_value(lamv, lam_init):
    s1 = jnp.sum(lamv[0:1] * lamv[1:2], axis=1, keepdims=True)
    s2 = jnp.sum(lamv[2:3] * lamv[3:4], axis=1, keepdims=True)
    return jnp.exp(s1) - jnp.exp(s2) + lam_init


def _bias_by_distance(rel_bias, n):
    dist = jnp.arange(n, dtype=jnp.int32)
    max_exact = N_BUCKETS // 2
    nf = jnp.maximum(dist, 1).astype(F32)
    large = max_exact + (jnp.log(nf / max_exact) / math.log(MAX_DISTANCE / max_exact)
                         * (N_BUCKETS - max_exact)).astype(jnp.int32)
    large = jnp.minimum(large, N_BUCKETS - 1)
    bucket = jnp.where(dist < max_exact, dist, large)
    return rel_bias[bucket].T.astype(F32)


def _attn_prompt_kernel(q_ref, k_ref, v_ref, bias_ref, lamv_ref, subln_ref, o_ref,
                        vt_ref, q2_ref, m_ref, l_ref, acc_ref, *, tq, tk, qc, lam_init):
    qi = pl.program_id(2)
    ratio = tq // tk
    n_special = bias_ref.shape[0]
    n_chunks = 2 * tq // qc

    @pl.when(qi == 0)
    def _():
        def transpose_tile(j, carry):
            start = pl.multiple_of(j * tk, tk)
            vt_ref[j] = v_ref[pl.ds(start, tk), :].T
            return carry
        lax.fori_loop(0, vt_ref.shape[0], transpose_tile, 0)

    qt = q_ref[...].T
    row = lax.broadcasted_iota(jnp.int32, qt.shape, 0)
    zero = jnp.zeros_like(qt)
    q2_ref[:, pl.ds(0, tq)] = jnp.where(row < HEAD_DIM, qt, zero)
    q2_ref[:, pl.ds(tq, tq)] = jnp.where(row >= HEAD_DIM, qt, zero)
    m_ref[...] = jnp.full(m_ref.shape, NEG_INF, F32)
    l_ref[...] = jnp.zeros(l_ref.shape, F32)
    acc_ref[...] = jnp.zeros(acc_ref.shape, F32)

    def step(kj, special):
        start = pl.multiple_of(kj * tk, tk)
        k = k_ref[pl.ds(start, tk), :]
        vt = vt_ref[kj]
        for c in range(n_chunks):
            q_lo = (c * qc) % tq
            if special is not None and q_lo + qc - 1 < (special - 1) * tk:
                continue
            cols = pl.ds(c * qc, qc)
            s = jnp.dot(k, q2_ref[:, cols], preferred_element_type=F32)
            if special is not None:
                s = s + bias_ref[special, :, pl.ds(q_lo, qc)]
            m_old = m_ref[:, cols]
            m_new = jnp.maximum(m_old, jnp.max(s, axis=0, keepdims=True))
            alpha = jnp.exp2(m_old - m_new)
            p = jnp.exp2(s - m_new)
            l_ref[:, cols] = alpha * l_ref[:, cols] + jnp.sum(p, axis=0, keepdims=True)
            pv = jnp.dot(vt, p.astype(BF16), preferred_element_type=F32)
            acc_ref[:, cols] = alpha * acc_ref[:, cols] + pv
            m_ref[:, cols] = m_new

    def run_tiles(first, specials):
        for t, special in enumerate(specials):
            step(first + t, special)

    n_far_blocks = jnp.maximum(qi - 1, 0)

    def far_group(g, carry):
        run_tiles(g * 2 * ratio, [None] * (2 * ratio))
        return carry

    lax.fori_loop(0, n_far_blocks // 2, far_group, 0)

    @pl.when(n_far_blocks % 2 == 1)
    def _():
        run_tiles((n_far_blocks - 1) * ratio, [None] * ratio)

    @pl.when(qi > 0)
    def _():
        run_tiles((qi - 1) * ratio, [None] * (ratio - 1) + list(range(n_special)))

    @pl.when(qi == 0)
    def _():
        run_tiles(0, list(range(1, n_special)))

    l = l_ref[...]
    acc = acc_ref[...]
    lam = _lambda_value(lamv_ref[...], lam_init)
    o = acc[:, :tq] / l[:, :tq] - lam * (acc[:, tq:] / l[:, tq:])
    ms = jnp.mean(o * o, axis=0, keepdims=True)
    y = o * lax.rsqrt(ms + NORM_EPS) * subln_ref[...] * (1.0 - lam_init)
    o_ref[...] = y.T.astype(BF16)


def _prompt_bias_tiles(bias_tab, tq, tk):
    n_special = tq // tk + 1
    rows = n_special * tk
    period = rows + tq
    x = jnp.arange(period, dtype=jnp.int32)
    dist = jnp.where(x < tq, x, x - period) + tk
    shifted = (bias_tab - bias_tab[:, -1:]) * LOG2E
    vals = jnp.where(dist < 0, NEG_INF, shifted[:, jnp.clip(dist, 0, bias_tab.shape[1] - 1)])
    flat = jnp.tile(vals, (1, rows))[:, :rows * (period - 1)]
    toeplitz = flat.reshape(-1, rows, period - 1)[:, :, :tq]
    return toeplitz.reshape(-1, n_special, tk, tq).astype(F32)


def _attn_prompt(qb, kb, vb, bias_tiles, lamv, subln_col, *, layer, tq, tk, qc, lam_init):
    b, t, w = qb.shape
    nk = t // tk
    n_special = bias_tiles.shape[1]
    grid = (b, N_HEADS, t // tq)
    return pl.pallas_call(
        functools.partial(_attn_prompt_kernel, tq=tq, tk=tk, qc=qc, lam_init=lam_init),
        grid=grid,
        in_specs=[
            pl.BlockSpec((None, tq, QK_DIM), lambda bi, h, qi: (bi, qi, h)),
            pl.BlockSpec((None, t, QK_DIM), lambda bi, h, qi: (bi, 0, h)),
            pl.BlockSpec((None, t, V_DIM), lambda bi, h, qi: (bi, 0, h)),
            pl.BlockSpec((None, n_special, tk, tq), lambda bi, h, qi: (h, 0, 0, 0)),
            pl.BlockSpec((None,) + lamv.shape[1:], lambda bi, h, qi: (layer, 0, 0)),
            pl.BlockSpec((None,) + subln_col.shape[1:], lambda bi, h, qi: (layer, 0, 0)),
        ],
        out_specs=pl.BlockSpec((None, tq, V_DIM), lambda bi, h, qi: (bi, qi, h)),
        out_shape=jax.ShapeDtypeStruct((b, t, w), BF16),
        scratch_shapes=[pltpu.VMEM((nk, V_DIM, tk), BF16),
                        pltpu.VMEM((QK_DIM, 2 * tq), BF16),
                        pltpu.VMEM((1, 2 * tq), F32),
                        pltpu.VMEM((1, 2 * tq), F32),
                        pltpu.VMEM((V_DIM, 2 * tq), F32)],
        compiler_params=pltpu.CompilerParams(
            dimension_semantics=("parallel", "parallel", "arbitrary"),
            vmem_limit_bytes=VMEM_LIMIT_BYTES),
        name="attn_prompt",
    )(qb, kb, vb, bias_tiles, lamv, subln_col)


def _attn_sample_kernel(pt_ref, q_ref, kn_ref, vn_ref, bias_ref, bias_self_ref, lamv_ref,
                        subln_ref, *rest, n_pages, seqs, lam_init):
    del pt_ref
    o_ref = rest[2 * seqs * n_pages]
    for i in range(seqs):
        kp_refs = rest[i * n_pages:(i + 1) * n_pages]
        vp_refs = rest[(seqs + i) * n_pages:(seqs + i + 1) * n_pages]
        o_ref[i] = _attn_sample_one(q_ref[i], kn_ref[i], vn_ref[i], bias_ref, bias_self_ref,
                                    lamv_ref, subln_ref, kp_refs, vp_refs, lam_init)


def _attn_sample_one(qh, kn, vn, bias_ref, bias_self_ref, lamv_ref, subln_ref, kp_refs, vp_refs,
                     lam_init):
    nt = (((1,), (1,)), ((), ()))
    half = PAGE_SIZE * N_HEADS // 2
    n_col = 2 * N_HEADS

    lane = lax.broadcasted_iota(jnp.int32, qh.shape, 1)
    zero = jnp.zeros_like(qh)
    q8 = jnp.concatenate([jnp.where(lane < HEAD_DIM, qh, zero),
                          jnp.where(lane >= HEAD_DIM, qh, zero)], axis=0)
    z8 = jnp.zeros_like(q8)
    rt = jnp.concatenate([jnp.concatenate([q8, z8], axis=1),
                          jnp.concatenate([z8, q8], axis=1)], axis=0)

    def paired(ref):
        return jnp.concatenate([ref[pl.ds(0, half), :], ref[pl.ds(half, half), :]], axis=1)

    s = jnp.concatenate([lax.dot_general(paired(kp), rt, nt, preferred_element_type=F32)
                         for kp in kp_refs], axis=0) + bias_ref[...]
    kn8 = jnp.concatenate([kn, kn], axis=0)
    s_self = lax.dot_general(jnp.concatenate([kn8, jnp.zeros_like(kn8)], axis=1), rt, nt,
                             preferred_element_type=F32) + bias_self_ref[...]

    m16 = jnp.maximum(jnp.max(s, axis=0, keepdims=True), jnp.max(s_self, axis=0, keepdims=True))
    m8 = jnp.maximum(m16[:, :n_col], m16[:, n_col:])
    m = jnp.concatenate([m8, m8], axis=1)
    p = jnp.exp(s - m)
    p_self = jnp.exp(s_self - m)
    l16 = jnp.sum(p, axis=0, keepdims=True) + jnp.sum(p_self, axis=0, keepdims=True)
    l8 = l16[:, :n_col] + l16[:, n_col:]
    lam = _lambda_value(lamv_ref[...], lam_init)
    col = lax.broadcasted_iota(jnp.int32, l8.shape, 1)
    coef8 = jnp.where(col < N_HEADS, 1.0, -lam) / l8
    coef = jnp.concatenate([coef8, coef8], axis=1)

    erow = lax.broadcasted_iota(jnp.int32, (2 * n_col, 2 * LANES), 0)
    elane = lax.broadcasted_iota(jnp.int32, (2 * n_col, 2 * LANES), 1)
    expand = ((erow < n_col) == (elane < LANES)).astype(F32)

    acc = jnp.zeros((half, V_DIM), F32)
    for j, vp in enumerate(vp_refs):
        wb = jnp.dot(p[j * half:(j + 1) * half] * coef, expand, preferred_element_type=F32)
        acc = acc + wb[:, :LANES] * vp[pl.ds(0, half), :] + wb[:, LANES:] * vp[pl.ds(half, half), :]
    acc8 = jnp.sum(acc.reshape(half // SUBLANES, SUBLANES, V_DIM), axis=0)
    wb_self = jnp.dot(p_self * coef, expand, preferred_element_type=F32)
    o = acc8[:N_HEADS] + acc8[N_HEADS:] + wb_self[:N_HEADS, :LANES] * vn

    ms = jnp.mean(o * o, axis=1, keepdims=True)
    return o * lax.rsqrt(ms + NORM_EPS) * subln_ref[...] * (1.0 - lam_init)


def _sample_bias_tables(bias_tab, n_pages):
    past_len = n_pages * PAGE_SIZE
    half_tok = PAGE_SIZE // 2
    rev = bias_tab[:, 1:past_len + 1][:, ::-1]
    vals = rev.reshape(N_HEADS, n_pages, 2, half_tok).transpose(1, 3, 0, 2)
    same_head = jnp.eye(N_HEADS, dtype=bool)
    full = jnp.where(same_head[None, None, :, None, None, :],
                     vals[:, :, :, :, None, None], NEG_INF)
    full = jnp.broadcast_to(full, (n_pages, half_tok, N_HEADS, 2, 2, N_HEADS))
    bias_past = full.reshape(n_pages * half_tok * N_HEADS, 4 * N_HEADS)
    self_vals = jnp.where(same_head[:, None, :], bias_tab[:, 0][:, None, None], NEG_INF)
    self_vals = jnp.broadcast_to(self_vals, (N_HEADS, 2, N_HEADS)).reshape(N_HEADS, 2 * N_HEADS)
    top = jnp.concatenate([self_vals, jnp.full_like(self_vals, NEG_INF)], axis=1)
    bias_self = jnp.concatenate([top, jnp.full_like(top, NEG_INF)], axis=0)
    return bias_past.astype(F32), bias_self.astype(F32)


def _attn_sample(page_table, q, k_new, v_new, cache_k, cache_v, bias_past, bias_self, lamv,
                 subln_row, *, layer, lam_init):
    n_seq, n_pages = page_table.shape
    seqs = SEQS_PER_STEP
    head_spec = pl.BlockSpec((seqs, N_HEADS, QK_DIM), lambda s, pt: (s, 0, 0))
    const2 = lambda a: pl.BlockSpec(a.shape, lambda s, pt: (0, 0))
    layered = lambda a: pl.BlockSpec((None,) + a.shape[1:], lambda s, pt: (layer, 0, 0))

    def page_spec(i, j):
        return pl.BlockSpec((None, None, PAGE_SIZE * N_HEADS, QK_DIM),
                            lambda s, pt: (layer, pt[(s * seqs + i) * n_pages + j], 0, 0))

    page_specs = [page_spec(i, j) for i in range(seqs) for j in range(n_pages)]
    grid_spec = pltpu.PrefetchScalarGridSpec(
        num_scalar_prefetch=1,
        grid=(n_seq // seqs,),
        in_specs=[head_spec, head_spec, head_spec, const2(bias_past), const2(bias_self),
                  layered(lamv), layered(subln_row)] + page_specs * 2,
        out_specs=head_spec,
    )
    n_page_args = seqs * n_pages
    return pl.pallas_call(
        functools.partial(_attn_sample_kernel, n_pages=n_pages, seqs=seqs, lam_init=lam_init),
        grid_spec=grid_spec,
        out_shape=jax.ShapeDtypeStruct(q.shape, F32),
        compiler_params=pltpu.CompilerParams(
            dimension_semantics=("arbitrary",), vmem_limit_bytes=VMEM_LIMIT_BYTES),
        name="attn_sample",
    )(page_table.reshape(-1), q, k_new, v_new, bias_past, bias_self, lamv, subln_row,
      *([cache_k] * n_page_args), *([cache_v] * n_page_args))


POOL_HALO = POOL_STATE + 1


def _pool_diff_prompt(p_ref, halo_ref, ext_ref, tile_index):
    tm = p_ref.shape[0]
    p = p_ref[...]
    ext_ref[pl.ds(0, POOL_HALO), :] = jnp.where(tile_index > 0, halo_ref[...], 0.0)
    ext_ref[pl.ds(POOL_HALO, tm), :] = p
    gd = p.shape[1] // len(POOL_WINDOWS)
    pos1 = tile_index * tm + lax.broadcasted_iota(jnp.int32, (tm, gd), 0) + 1
    diffs = []
    for g, w in enumerate(POOL_WINDOWS):
        lanes = pl.ds(g * gd, gd)
        ws = ext_ref[pl.ds(POOL_HALO, tm), lanes]
        for j in range(1, w):
            ws = ws + ext_ref[pl.ds(POOL_HALO - j, tm), lanes]
        cnt = jnp.minimum(pos1, w).astype(F32)
        diffs.append(ws / cnt - p[:, g * gd:(g + 1) * gd])
    return diffs


def _pool_sample_kernel(state_ref, p_ref, d_ref, new_ref, *, past_len):
    w_tot = p_ref.shape[1]
    gd = w_tot // len(POOL_WINDOWS)
    p = p_ref[...]
    for g, w in enumerate(POOL_WINDOWS):
        ws = p[:, g * gd:(g + 1) * gd]
        for j in range(1, w):
            row = POOL_STATE - j
            ws = ws + state_ref[:, pl.ds(row * w_tot + g * gd, gd)]
        cnt = float(min(past_len + 1, w))
        d_ref[:, pl.ds(g * gd, gd)] = ws / cnt - p[:, g * gd:(g + 1) * gd]
    keep = (POOL_STATE - 1) * w_tot
    new_ref[:, pl.ds(0, keep)] = state_ref[:, pl.ds(w_tot, keep)]
    new_ref[:, pl.ds(keep, w_tot)] = p


def _pool_sample(state, p, *, past_len):
    return pl.pallas_call(
        functools.partial(_pool_sample_kernel, past_len=past_len),
        out_shape=[jax.ShapeDtypeStruct(p.shape, F32), jax.ShapeDtypeStruct(state.shape, F32)],
        name="pool_sample",
    )(state, p)


def _merge_out_body(h_ref, o_ref, diffs, mpre_ref, wgate_ref, poolw_ref, pscale_ref, wa_ref,
                    wb_ref, wout_ref, mpost_ref, pre2_ref, post2_ref, wgu_ref, wd_ref, y_ref, d_ff):
    h = h_ref[...]
    d_model = h.shape[1]
    u = _rms(h, mpre_ref[...]).astype(BF16)
    gates = jnp.dot(u, wgate_ref[...], preferred_element_type=F32)
    pooled = jnp.concatenate(
        [jnp.dot(d.astype(BF16), poolw_ref[g], preferred_element_type=F32)
         for g, d in enumerate(diffs)], axis=1) * pscale_ref[...]
    branch_a = jnp.dot(o_ref[...].astype(BF16), wa_ref[...], preferred_element_type=F32)
    branch_b = jnp.dot(pooled.astype(BF16), wb_ref[...], preferred_element_type=F32)
    merged = (jax.nn.sigmoid(gates[:, :d_model]) * branch_a
              + jax.nn.sigmoid(gates[:, d_model:]) * branch_b)
    mixed = jnp.dot(merged.astype(BF16), wout_ref[...], preferred_element_type=F32)
    h2 = h + _rms(mixed, mpost_ref[...])
    f = _swiglu(_rms(h2, pre2_ref[...]).astype(BF16), wgu_ref, wd_ref, d_ff)
    y_ref[...] = h2 + 0.5 * _rms(f, post2_ref[...])


def _merge_out_prompt_kernel(h_ref, o_ref, p_ref, halo_ref, *rest, d_ff):
    *param_refs, y_ref, ext_ref = rest
    diffs = _pool_diff_prompt(p_ref, halo_ref, ext_ref, pl.program_id(1))
    _merge_out_body(h_ref, o_ref, diffs, *param_refs, y_ref, d_ff)


def _merge_out_prompt(h, o, p, weights, *, layer, tm):
    b, t, dm = h.shape
    w = p.shape[-1]
    tok = lambda width: pl.BlockSpec((None, tm, width), lambda bi, i: (bi, i, 0))
    per = tm // POOL_HALO
    halo = pl.BlockSpec((None, POOL_HALO, w), lambda bi, i: (bi, jnp.maximum(i * per - 1, 0), 0))
    return pl.pallas_call(
        functools.partial(_merge_out_prompt_kernel, d_ff=weights[-1].shape[1]),
        grid=(b, t // tm),
        in_specs=[tok(dm), tok(w), tok(w), halo] + [_layer_spec(a, layer) for a in weights],
        out_specs=tok(dm),
        out_shape=jax.ShapeDtypeStruct(h.shape, F32),
        scratch_shapes=[pltpu.VMEM((tm + POOL_HALO, w), F32)],
        compiler_params=pltpu.CompilerParams(
            dimension_semantics=("parallel", "parallel"), vmem_limit_bytes=VMEM_LIMIT_BYTES),
        name="merge_out_prompt",
    )(h, o, p, p, *weights)


def _merge_out_sample_kernel(h_ref, o_ref, d_ref, *rest, d_ff):
    *param_refs, y_ref = rest
    gd = d_ref.shape[1] // len(POOL_WINDOWS)
    diffs = [d_ref[:, pl.ds(g * gd, gd)] for g in range(len(POOL_WINDOWS))]
    _merge_out_body(h_ref, o_ref, diffs, *param_refs, y_ref, d_ff)


def _merge_out_sample(h, o, d, weights, *, layer):
    n, dm = h.shape
    whole = lambda a: pl.BlockSpec(a.shape, lambda i: (0, 0))
    return pl.pallas_call(
        functools.partial(_merge_out_sample_kernel, d_ff=weights[-1].shape[1]),
        grid=(1,),
        in_specs=[whole(h), whole(o), whole(d)] + [_layer_spec(a, layer) for a in weights],
        out_specs=whole(h),
        out_shape=jax.ShapeDtypeStruct(h.shape, F32),
        compiler_params=pltpu.CompilerParams(
            dimension_semantics=("arbitrary",), vmem_limit_bytes=VMEM_LIMIT_BYTES),
        name="merge_out_sample",
    )(h, o, d, *weights)


TM_PROMPT = 256
TQ = 512
TK = 128
QC = 256
SEQS_PER_STEP = 2


def kernel(x_prompt, x_sample, cache_k, cache_v, state_pool, page_table, rel_bias, ffn1_norm_pre, ffn1_norm_post, ffn1_w_gate, ffn1_w_up, ffn1_w_down, mix_norm_pre, mix_norm_post, w_in, lambda_q1, lambda_k1, lambda_q2, lambda_k2, attn_subln, pool_w, pool_scale, w_branch_a, w_branch_b, w_out, ffn2_norm_pre, ffn2_norm_post, ffn2_w_gate, ffn2_w_up, ffn2_w_down):
    depth = w_in.shape[0]
    bsz, seq, d_model = x_prompt.shape
    n_seq = x_sample.shape[0]
    n_pages = page_table.shape[1]
    past_len = n_pages * PAGE_SIZE
    n_phys = cache_k.shape[1]
    w = ATTN_W
    qkvp_w = 4 * w

    wgu1 = jnp.concatenate([ffn1_w_gate, ffn1_w_up], axis=-1).astype(BF16)
    wgu2 = jnp.concatenate([ffn2_w_gate, ffn2_w_up], axis=-1).astype(BF16)
    wd1 = ffn1_w_down.astype(BF16)
    wd2 = ffn2_w_down.astype(BF16)
    w_qkvp = w_in[:, :, :qkvp_w].astype(BF16)
    w_gate = w_in[:, :, qkvp_w:].astype(BF16)
    wa = w_branch_a.astype(BF16)
    wb = w_branch_b.astype(BF16)
    wo = w_out.astype(BF16)
    pw = pool_w.astype(BF16)
    row = lambda a: a[:, None, :]
    lamv = jnp.stack([lambda_q1, lambda_k1, lambda_q2, lambda_k2], axis=1)
    in_w = (row(ffn1_norm_pre), row(ffn1_norm_post), wgu1, wd1, row(mix_norm_pre), w_qkvp)
    out_w = (row(mix_norm_pre), w_gate, pw, row(pool_scale), wa, wb, wo,
             row(mix_norm_post), row(ffn2_norm_pre), row(ffn2_norm_post), wgu2, wd2)
    subln_col = attn_subln[:, :, None]
    subln_row = attn_subln[:, None, :]

    bias_tab = _bias_by_distance(rel_bias, max(TQ + TK, past_len + 1))
    bias_tiles = _prompt_bias_tiles(bias_tab[:, :TQ + TK], TQ, TK)
    bias_past, bias_self = _sample_bias_tables(bias_tab, n_pages)
    ck = cache_k.reshape(depth, n_phys, PAGE_SIZE * N_HEADS, QK_DIM)
    cv = cache_v.reshape(depth, n_phys, PAGE_SIZE * N_HEADS, V_DIM)

    xp = x_prompt
    xs = x_sample.reshape(n_seq, d_model)
    k_prompt = jnp.zeros((depth, bsz, seq, N_HEADS, QK_DIM), F32)
    v_prompt = jnp.zeros((depth, bsz, seq, N_HEADS, V_DIM), F32)
    outs = [[] for _ in range(4)]
    for l in range(depth):
        lam_init = 0.8 - 0.6 * math.exp(-0.3 * l)
        h, p, qb, kb, vb, k_prompt, v_prompt = _ffn_in_prompt(
            xp, in_w, k_prompt, v_prompt, layer=l, tm=TM_PROMPT)
        o = _attn_prompt(qb, kb, vb, bias_tiles, lamv, subln_col,
                         layer=l, tq=TQ, tk=TK, qc=QC, lam_init=lam_init)
        xp = _merge_out_prompt(h, o, p, out_w, layer=l, tm=TM_PROMPT)
        outs[0].append(p[:, seq - POOL_STATE:, :])

        hs, qs, ks, vs, ps = _ffn_in_sample(xs, in_w, layer=l)
        by_head = lambda a: a.reshape(n_seq, N_HEADS, QK_DIM)
        o_s = _attn_sample(page_table, by_head(qs), by_head(ks), by_head(vs), ck, cv,
                           bias_past, bias_self, lamv, subln_row, layer=l, lam_init=lam_init)
        d_s, new_state = _pool_sample(state_pool[l].reshape(n_seq, POOL_STATE * w), ps,
                                      past_len=past_len)
        xs = _merge_out_sample(hs, o_s.reshape(n_seq, w), d_s, out_w, layer=l)
        outs[1].append(ks.reshape(n_seq, 1, N_HEADS, QK_DIM))
        outs[2].append(vs.reshape(n_seq, 1, N_HEADS, V_DIM))
        outs[3].append(new_state.reshape(n_seq, POOL_STATE, w))

    stacked = [jnp.stack(o) for o in outs]
    return (xp, xs.reshape(n_seq, 1, d_model), k_prompt, v_prompt, stacked[0],
            stacked[1], stacked[2], stacked[3])
```

```python
import functools
import math

import jax
import jax.numpy as jnp
from jax import lax
from jax.experimental import pallas as pl
from jax.experimental.pallas import tpu as pltpu

F32 = jnp.float32
BF16 = jnp.bfloat16

N_HEADS = 4
HEAD_DIM = 64
QK_DIM = 2 * HEAD_DIM
V_DIM = 2 * HEAD_DIM
ATTN_W = N_HEADS * V_DIM
POOL_WINDOWS = (2, 4, 8, 16)
POOL_STATE = max(POOL_WINDOWS) - 1
N_BUCKETS = 32
MAX_DISTANCE = 128
PAGE_SIZE = 128
NORM_EPS = 1e-6
NEG_INF = -1e30
SCALE = HEAD_DIM ** -0.5
LOG2E = math.log2(math.e)

LANES = 128
SUBLANES = 8
VMEM_LIMIT_BYTES = 56 * 1024 * 1024
MXU_TILE = 256
FF_CHUNK = 6 * MXU_TILE


def _rms(x, g):
    ms = jnp.mean(x * x, axis=-1, keepdims=True)
    return x * lax.rsqrt(ms + NORM_EPS) * g


def _layer_spec(stacked, layer):
    zeros = (0,) * (stacked.ndim - 1)
    return pl.BlockSpec((None,) + stacked.shape[1:], lambda *_: (layer,) + zeros,
                        pipeline_mode=pl.Buffered(1))


def _swiglu(u, wg_ref, wu_ref, wd_ref, d_ff):
    f = None
    for lo in range(0, d_ff, FF_CHUNK):
        hi = min(lo + FF_CHUNK, d_ff)
        g = jnp.dot(u, wg_ref[:, lo:hi], preferred_element_type=F32)
        up = jnp.dot(u, wu_ref[:, lo:hi], preferred_element_type=F32)
        act = (g * jax.nn.sigmoid(g) * up).astype(BF16)
        part = jnp.dot(act, wd_ref[lo:hi, :], preferred_element_type=F32)
        f = part if f is None else f + part
    return f


def _ffn_in_body(x_ref, pre_ref, post_ref, wg_ref, wu_ref, wd_ref, mpre_ref, win_ref, d_ff):
    x = x_ref[...]
    f = _swiglu(_rms(x, pre_ref[...]).astype(BF16), wg_ref, wu_ref, wd_ref, d_ff)
    h = x + 0.5 * _rms(f, post_ref[...])
    u = _rms(h, mpre_ref[...]).astype(BF16)
    proj = jnp.dot(u, win_ref[...], preferred_element_type=F32)
    w = ATTN_W
    return h, proj[:, :w], proj[:, w:2 * w], proj[:, 2 * w:3 * w], proj[:, 3 * w:]


def _ffn_in_prompt_kernel(x_ref, *rest, d_ff):
    *param_refs, kbuf_ref, vbuf_ref, h_ref, p_ref, qb_ref, kb_ref, vb_ref, k5_ref, v5_ref = rest
    del kbuf_ref, vbuf_ref
    h, q, k, v, p = _ffn_in_body(x_ref, *param_refs, d_ff)
    h_ref[...] = h
    p_ref[...] = p
    qb_ref[...] = (q * (SCALE * LOG2E)).astype(BF16)
    kb_ref[...] = k.astype(BF16)
    vb_ref[...] = v.astype(BF16)
    for hd in range(N_HEADS):
        k5_ref[:, hd, :] = k[:, hd * QK_DIM:(hd + 1) * QK_DIM]
        v5_ref[:, hd, :] = v[:, hd * V_DIM:(hd + 1) * V_DIM]


def _ffn_in_prompt(x, weights, kbuf, vbuf, *, layer, tm):
    b, t, d = x.shape
    d_ff = weights[4].shape[1]
    w = ATTN_W
    tok = lambda width: pl.BlockSpec((None, tm, width), lambda bi, i: (bi, i, 0))
    kv5 = pl.BlockSpec((None, None, tm, N_HEADS, QK_DIM), lambda bi, i: (layer, bi, i, 0, 0))
    any_spec = pl.BlockSpec(memory_space=pl.ANY)
    n_in = 1 + len(weights)
    return pl.pallas_call(
        functools.partial(_ffn_in_prompt_kernel, d_ff=d_ff),
        grid=(b, t // tm),
        in_specs=[tok(d)] + [_layer_spec(a, layer) for a in weights] + [any_spec, any_spec],
        out_specs=[tok(d), tok(w), tok(w), tok(w), tok(w), kv5, kv5],
        out_shape=[jax.ShapeDtypeStruct((b, t, d), F32),
                   jax.ShapeDtypeStruct((b, t, w), F32),
                   jax.ShapeDtypeStruct((b, t, w), BF16),
                   jax.ShapeDtypeStruct((b, t, w), BF16),
                   jax.ShapeDtypeStruct((b, t, w), BF16),
                   jax.ShapeDtypeStruct(kbuf.shape, F32),
                   jax.ShapeDtypeStruct(vbuf.shape, F32)],
        input_output_aliases={n_in: 5, n_in + 1: 6},
        compiler_params=pltpu.CompilerParams(
            dimension_semantics=("parallel", "parallel"), vmem_limit_bytes=VMEM_LIMIT_BYTES),
        name="ffn_in_prompt",
    )(x, *weights, kbuf, vbuf)


def _ffn_in_sample_kernel(x_ref, *rest, d_ff):
    *param_refs, h_ref, q_ref, k_ref, v_ref, p_ref = rest
    h, q, k, v, p = _ffn_in_body(x_ref, *param_refs, d_ff)
    h_ref[...] = h
    q_ref[...] = q * SCALE
    k_ref[...] = k
    v_ref[...] = v
    p_ref[...] = p


def _ffn_in_sample(x, weights, *, layer):
    n, d = x.shape
    d_ff = weights[4].shape[1]
    whole = lambda width: pl.BlockSpec((n, width), lambda i: (0, 0))
    return pl.pallas_call(
        functools.partial(_ffn_in_sample_kernel, d_ff=d_ff),
        grid=(1,),
        in_specs=[whole(d)] + [_layer_spec(a, layer) for a in weights],
        out_specs=[whole(d)] + [whole(ATTN_W)] * 4,
        out_shape=[jax.ShapeDtypeStruct((n, d), F32)] + [jax.ShapeDtypeStruct((n, ATTN_W), F32)] * 4,
        compiler_params=pltpu.CompilerParams(
            dimension_semantics=("arbitrary",), vmem_limit_bytes=VMEM_LIMIT_BYTES),
        name="ffn_in_sample",
    )(x, *weights)


def _lambda_value(lamv, lam_init):
    s1 = jnp.sum(lamv[0:1] * lamv[1:2], axis=1, keepdims=True)
    s2 = jnp.sum(lamv[2:3] * lamv[3:4], axis=1, keepdims=True)
    return jnp.exp(s1) - jnp.exp(s2) + lam_init


def _bias_by_distance(rel_bias, n):
    dist = jnp.arange(n, dtype=jnp.int32)
    max_exact = N_BUCKETS // 2
    nf = jnp.maximum(dist, 1).astype(F32)
    large = max_exact + (jnp.log(nf / max_exact) / math.log(MAX_DISTANCE / max_exact)
                         * (N_BUCKETS - max_exact)).astype(jnp.int32)
    large = jnp.minimum(large, N_BUCKETS - 1)
    bucket = jnp.where(dist < max_exact, dist, large)
    return rel_bias[bucket].T.astype(F32)


def _attn_prompt_kernel(q_ref, k_ref, v_ref, bias_ref, lamv_ref, subln_ref, o_ref,
                        vt_ref, q2_ref, m_ref, l_ref, acc_ref, *, tq, tk, qc, lam_init):
    qi = pl.program_id(2)
    ratio = tq // tk
    n_special = bias_ref.shape[0]
    n_chunks = 2 * tq // qc

    @pl.when(qi == 0)
    def _():
        def transpose_tile(j, carry):
            start = pl.multiple_of(j * tk, tk)
            vt_ref[j] = v_ref[pl.ds(start, tk), :].T
            return carry
        lax.fori_loop(0, vt_ref.shape[0], transpose_tile, 0)

    qt = q_ref[...].T
    row = lax.broadcasted_iota(jnp.int32, qt.shape, 0)
    zero = jnp.zeros_like(qt)
    q2_ref[:, pl.ds(0, tq)] = jnp.where(row < HEAD_DIM, qt, zero)
    q2_ref[:, pl.ds(tq, tq)] = jnp.where(row >= HEAD_DIM, qt, zero)
    m_ref[...] = jnp.full(m_ref.shape, NEG_INF, F32)
    l_ref[...] = jnp.zeros(l_ref.shape, F32)
    acc_ref[...] = jnp.zeros(acc_ref.shape, F32)

    def step(kj, special):
        start = pl.multiple_of(kj * tk, tk)
        k = k_ref[pl.ds(start, tk), :]
        vt = vt_ref[kj]
        for c in range(n_chunks):
            q_lo = (c * qc) % tq
            if special is not None and q_lo + qc - 1 < (special - 1) * tk:
                continue
            cols = pl.ds(c * qc, qc)
            s = jnp.dot(k, q2_ref[:, cols], preferred_element_type=F32)
            if special is not None:
                s = s + bias_ref[special, :, pl.ds(q_lo, qc)]
            m_old = m_ref[:, cols]
            m_new = jnp.maximum(m_old, jnp.max(s, axis=0, keepdims=True))
            alpha = jnp.exp2(m_old - m_new)
            p = jnp.exp2(s - m_new)
            l_ref[:, cols] = alpha * l_ref[:, cols] + jnp.sum(p, axis=0, keepdims=True)
            pv = jnp.dot(vt, p.astype(BF16), preferred_element_type=F32)
            acc_ref[:, cols] = alpha * acc_ref[:, cols] + pv
            m_ref[:, cols] = m_new

    def run_tiles(first, specials):
        for t, special in enumerate(specials):
            step(first + t, special)

    n_far_blocks = jnp.maximum(qi - 1, 0)

    def far_group(g, carry):
        run_tiles(g * 2 * ratio, [None] * (2 * ratio))
        return carry

    lax.fori_loop(0, n_far_blocks // 2, far_group, 0)

    @pl.when(n_far_blocks % 2 == 1)
    def _():
        run_tiles((n_far_blocks - 1) * ratio, [None] * ratio)

    @pl.when(qi > 0)
    def _():
        run_tiles((qi - 1) * ratio, [None] * (ratio - 1) + list(range(n_special)))

    @pl.when(qi == 0)
    def _():
        run_tiles(0, list(range(1, n_special)))

    l = l_ref[...]
    acc = acc_ref[...]
    lam = _lambda_value(lamv_ref[...], lam_init)
    o = acc[:, :tq] / l[:, :tq] - lam * (acc[:, tq:] / l[:, tq:])
    ms = jnp.mean(o * o, axis=0, keepdims=True)
    y = o * lax.rsqrt(ms + NORM_EPS) * subln_ref[...] * (1.0 - lam_init)
    o_ref[...] = y.T.astype(BF16)


def _prompt_bias_tiles(bias_tab, tq, tk):
    n_special = tq // tk + 1
    rows = n_special * tk
    period = rows + tq
    x = jnp.arange(period, dtype=jnp.int32)
    dist = jnp.where(x < tq, x, x - period) + tk
    shifted = (bias_tab - bias_tab[:, -1:]) * LOG2E
    vals = jnp.where(dist < 0, NEG_INF, shifted[:, jnp.clip(dist, 0, bias_tab.shape[1] - 1)])
    flat = jnp.tile(vals, (1, rows))[:, :rows * (period - 1)]
    toeplitz = flat.reshape(-1, rows, period - 1)[:, :, :tq]
    return toeplitz.reshape(-1, n_special, tk, tq).astype(F32)


def _attn_prompt(qb, kb, vb, bias_tiles, lamv, subln_col, *, layer, tq, tk, qc, lam_init):
    b, t, w = qb.shape
    nk = t // tk
    n_special = bias_tiles.shape[1]
    grid = (b, N_HEADS, t // tq)
    return pl.pallas_call(
        functools.partial(_attn_prompt_kernel, tq=tq, tk=tk, qc=qc, lam_init=lam_init),
        grid=grid,
        in_specs=[
            pl.BlockSpec((None, tq, QK_DIM), lambda bi, h, qi: (bi, qi, h)),
            pl.BlockSpec((None, t, QK_DIM), lambda bi, h, qi: (bi, 0, h)),
            pl.BlockSpec((None, t, V_DIM), lambda bi, h, qi: (bi, 0, h)),
            pl.BlockSpec((None, n_special, tk, tq), lambda bi, h, qi: (h, 0, 0, 0)),
            pl.BlockSpec((None,) + lamv.shape[1:], lambda bi, h, qi: (layer, 0, 0)),
            pl.BlockSpec((None,) + subln_col.shape[1:], lambda bi, h, qi: (layer, 0, 0)),
        ],
        out_specs=pl.BlockSpec((None, tq, V_DIM), lambda bi, h, qi: (bi, qi, h)),
        out_shape=jax.ShapeDtypeStruct((b, t, w), BF16),
        scratch_shapes=[pltpu.VMEM((nk, V_DIM, tk), BF16),
                        pltpu.VMEM((QK_DIM, 2 * tq), BF16),
                        pltpu.VMEM((1, 2 * tq), F32),
                        pltpu.VMEM((1, 2 * tq), F32),
                        pltpu.VMEM((V_DIM, 2 * tq), F32)],
        compiler_params=pltpu.CompilerParams(
            dimension_semantics=("parallel", "parallel", "arbitrary"),
            vmem_limit_bytes=VMEM_LIMIT_BYTES),
        name="attn_prompt",
    )(qb, kb, vb, bias_tiles, lamv, subln_col)


def _attn_sample_kernel(pt_ref, q_ref, kn_ref, vn_ref, bias_ref, bias_self_ref, lamv_ref,
                        subln_ref, *rest, n_pages, seqs, lam_init):
    del pt_ref
    o_ref = rest[2 * seqs * n_pages]
    for i in range(seqs):
        kp_refs = rest[i * n_pages:(i + 1) * n_pages]
        vp_refs = rest[(seqs + i) * n_pages:(seqs + i + 1) * n_pages]
        o_ref[i] = _attn_sample_one(q_ref[i], kn_ref[i], vn_ref[i], bias_ref, bias_self_ref,
                                    lamv_ref, subln_ref, kp_refs, vp_refs, lam_init)


def _attn_sample_one(qh, kn, vn, bias_ref, bias_self_ref, lamv_ref, subln_ref, kp_refs, vp_refs,
                     lam_init):
    nt = (((1,), (1,)), ((), ()))
    half = PAGE_SIZE * N_HEADS // 2
    n_col = 2 * N_HEADS

    lane = lax.broadcasted_iota(jnp.int32, qh.shape, 1)
    zero = jnp.zeros_like(qh)
    q8 = jnp.concatenate([jnp.where(lane < HEAD_DIM, qh, zero),
                          jnp.where(lane >= HEAD_DIM, qh, zero)], axis=0)
    z8 = jnp.zeros_like(q8)
    rt = jnp.concatenate([jnp.concatenate([q8, z8], axis=1),
                          jnp.concatenate([z8, q8], axis=1)], axis=0)

    def paired(ref):
        return jnp.concatenate([ref[pl.ds(0, half), :], ref[pl.ds(half, half), :]], axis=1)

    s = jnp.concatenate([lax.dot_general(paired(kp), rt, nt, preferred_element_type=F32)
                         for kp in kp_refs], axis=0) + bias_ref[...]
    kn8 = jnp.concatenate([kn, kn], axis=0)
    s_self = lax.dot_general(jnp.concatenate([kn8, jnp.zeros_like(kn8)], axis=1), rt, nt,
                             preferred_element_type=F32) + bias_self_ref[...]

    m16 = jnp.maximum(jnp.max(s, axis=0, keepdims=True), jnp.max(s_self, axis=0, keepdims=True))
    m8 = jnp.maximum(m16[:, :n_col], m16[:, n_col:])
    m = jnp.concatenate([m8, m8], axis=1)
    p = jnp.exp(s - m)
    p_self = jnp.exp(s_self - m)
    l16 = jnp.sum(p, axis=0, keepdims=True) + jnp.sum(p_self, axis=0, keepdims=True)
    l8 = l16[:, :n_col] + l16[:, n_col:]
    lam = _lambda_value(lamv_ref[...], lam_init)
    col = lax.broadcasted_iota(jnp.int32, l8.shape, 1)
    coef8 = jnp.where(col < N_HEADS, 1.0, -lam) / l8
    coef = jnp.concatenate([coef8, coef8], axis=1)

    erow = lax.broadcasted_iota(jnp.int32, (2 * n_col, 2 * LANES), 0)
    elane = lax.broadcasted_iota(jnp.int32, (2 * n_col, 2 * LANES), 1)
    expand = ((erow < n_col) == (elane < LANES)).astype(F32)

    acc = jnp.zeros((half, V_DIM), F32)
    for j, vp in enumerate(vp_refs):
        wb = jnp.dot(p[j * half:(j + 1) * half] * coef, expand, preferred_element_type=F32)
        acc = acc + wb[:, :LANES] * vp[pl.ds(0, half), :] + wb[:, LANES:] * vp[pl.ds(half, half), :]
    acc8 = jnp.sum(acc.reshape(half // SUBLANES, SUBLANES, V_DIM), axis=0)
    wb_self = jnp.dot(p_self * coef, expand, preferred_element_type=F32)
    o = acc8[:N_HEADS] + acc8[N_HEADS:] + wb_self[:N_HEADS, :LANES] * vn

    ms = jnp.mean(o * o, axis=1, keepdims=True)
    return o * lax.rsqrt(ms + NORM_EPS) * subln_ref[...] * (1.0 - lam_init)


def _sample_bias_tables(bias_tab, n_pages):
    past_len = n_pages * PAGE_SIZE
    half_tok = PAGE_SIZE // 2
    rev = bias_tab[:, 1:past_len + 1][:, ::-1]
    vals = rev.reshape(N_HEADS, n_pages, 2, half_tok).transpose(1, 3, 0, 2)
    same_head = jnp.eye(N_HEADS, dtype=bool)
    full = jnp.where(same_head[None, None, :, None, None, :],
                     vals[:, :, :, :, None, None], NEG_INF)
    full = jnp.broadcast_to(full, (n_pages, half_tok, N_HEADS, 2, 2, N_HEADS))
    bias_past = full.reshape(n_pages * half_tok * N_HEADS, 4 * N_HEADS)
    self_vals = jnp.where(same_head[:, None, :], bias_tab[:, 0][:, None, None], NEG_INF)
    self_vals = jnp.broadcast_to(self_vals, (N_HEADS, 2, N_HEADS)).reshape(N_HEADS, 2 * N_HEADS)
    top = jnp.concatenate([self_vals, jnp.full_like(self_vals, NEG_INF)], axis=1)
    bias_self = jnp.concatenate([top, jnp.full_like(top, NEG_INF)], axis=0)
    return bias_past.astype(F32), bias_self.astype(F32)


def _attn_sample(page_table, q, k_new, v_new, cache_k, cache_v, bias_past, bias_self, lamv,
                 subln_row, *, layer, lam_init):
    n_seq, n_pages = page_table.shape
    seqs = SEQS_PER_STEP
    head_spec = pl.BlockSpec((seqs, N_HEADS, QK_DIM), lambda s, pt: (s, 0, 0))
    const2 = lambda a: pl.BlockSpec(a.shape, lambda s, pt: (0, 0))
    layered = lambda a: pl.BlockSpec((None,) + a.shape[1:], lambda s, pt: (layer, 0, 0))

    def page_spec(i, j):
        return pl.BlockSpec((None, None, PAGE_SIZE * N_HEADS, QK_DIM),
                            lambda s, pt: (layer, pt[(s * seqs + i) * n_pages + j], 0, 0))

    page_specs = [page_spec(i, j) for i in range(seqs) for j in range(n_pages)]
    grid_spec = pltpu.PrefetchScalarGridSpec(
        num_scalar_prefetch=1,
        grid=(n_seq // seqs,),
        in_specs=[head_spec, head_spec, head_spec, const2(bias_past), const2(bias_self),
                  layered(lamv), layered(subln_row)] + page_specs * 2,
        out_specs=head_spec,
    )
    n_page_args = seqs * n_pages
    return pl.pallas_call(
        functools.partial(_attn_sample_kernel, n_pages=n_pages, seqs=seqs, lam_init=lam_init),
        grid_spec=grid_spec,
        out_shape=jax.ShapeDtypeStruct(q.shape, F32),
        compiler_params=pltpu.CompilerParams(
            dimension_semantics=("arbitrary",), vmem_limit_bytes=VMEM_LIMIT_BYTES),
        name="attn_sample",
    )(page_table.reshape(-1), q, k_new, v_new, bias_past, bias_self, lamv, subln_row,
      *([cache_k] * n_page_args), *([cache_v] * n_page_args))


POOL_HALO = POOL_STATE + 1


def _pool_diff_prompt(p_ref, halo_ref, ext_ref, tile_index):
    tm = p_ref.shape[0]
    p = p_ref[...]
    ext_ref[pl.ds(0, POOL_HALO), :] = jnp.where(tile_index > 0, halo_ref[...], 0.0)
    ext_ref[pl.ds(POOL_HALO, tm), :] = p
    gd = p.shape[1] // len(POOL_WINDOWS)
    pos1 = tile_index * tm + lax.broadcasted_iota(jnp.int32, (tm, gd), 0) + 1
    diffs = []
    for g, w in enumerate(POOL_WINDOWS):
        lanes = pl.ds(g * gd, gd)
        ws = ext_ref[pl.ds(POOL_HALO, tm), lanes]
        for j in range(1, w):
            ws = ws + ext_ref[pl.ds(POOL_HALO - j, tm), lanes]
        cnt = jnp.minimum(pos1, w).astype(F32)
        diffs.append(ws / cnt - p[:, g * gd:(g + 1) * gd])
    return diffs


def _pool_sample_kernel(state_ref, p_ref, d_ref, new_ref, *, past_len):
    w_tot = p_ref.shape[1]
    gd = w_tot // len(POOL_WINDOWS)
    p = p_ref[...]
    for g, w in enumerate(POOL_WINDOWS):
        ws = p[:, g * gd:(g + 1) * gd]
        for j in range(1, w):
            row = POOL_STATE - j
            ws = ws + state_ref[:, pl.ds(row * w_tot + g * gd, gd)]
        cnt = float(min(past_len + 1, w))
        d_ref[:, pl.ds(g * gd, gd)] = ws / cnt - p[:, g * gd:(g + 1) * gd]
    keep = (POOL_STATE - 1) * w_tot
    new_ref[:, pl.ds(0, keep)] = state_ref[:, pl.ds(w_tot, keep)]
    new_ref[:, pl.ds(keep, w_tot)] = p


def _pool_sample(state, p, *, past_len):
    return pl.pallas_call(
        functools.partial(_pool_sample_kernel, past_len=past_len),
        out_shape=[jax.ShapeDtypeStruct(p.shape, F32), jax.ShapeDtypeStruct(state.shape, F32)],
        name="pool_sample",
    )(state, p)


def _merge_out_body(h_ref, o_ref, diffs, mpre_ref, wgate_ref, poolw_ref, pscale_ref, wa_ref,
                    wb_ref, wout_ref, mpost_ref, pre2_ref, post2_ref, wg_ref, wu_ref, wd_ref, y_ref,
                    d_ff):
    h = h_ref[...]
    d_model = h.shape[1]
    u = _rms(h, mpre_ref[...]).astype(BF16)
    gates = jnp.dot(u, wgate_ref[...], preferred_element_type=F32)
    pooled = jnp.concatenate(
        [jnp.dot(d.astype(BF16), poolw_ref[g], preferred_element_type=F32)
         for g, d in enumerate(diffs)], axis=1) * pscale_ref[...]
    branch_a = jnp.dot(o_ref[...].astype(BF16), wa_ref[...], preferred_element_type=F32)
    branch_b = jnp.dot(pooled.astype(BF16), wb_ref[...], preferred_element_type=F32)
    merged = (jax.nn.sigmoid(gates[:, :d_model]) * branch_a
              + jax.nn.sigmoid(gates[:, d_model:]) * branch_b)
    mixed = jnp.dot(merged.astype(BF16), wout_ref[...], preferred_element_type=F32)
    h2 = h + _rms(mixed, mpost_ref[...])
    f = _swiglu(_rms(h2, pre2_ref[...]).astype(BF16), wg_ref, wu_ref, wd_ref, d_ff)
    y_ref[...] = h2 + 0.5 * _rms(f, post2_ref[...])


def _merge_out_prompt_kernel(h_ref, o_ref, p_ref, halo_ref, *rest, d_ff):
    *param_refs, y_ref, ext_ref = rest
    diffs = _pool_diff_prompt(p_ref, halo_ref, ext_ref, pl.program_id(1))
    _merge_out_body(h_ref, o_ref, diffs, *param_refs, y_ref, d_ff)


def _merge_out_prompt(h, o, p, weights, *, layer, tm):
    b, t, dm = h.shape
    w = p.shape[-1]
    tok = lambda width: pl.BlockSpec((None, tm, width), lambda bi, i: (bi, i, 0))
    per = tm // POOL_HALO
    halo = pl.BlockSpec((None, POOL_HALO, w), lambda bi, i: (bi, jnp.maximum(i * per - 1, 0), 0))
    return pl.pallas_call(
        functools.partial(_merge_out_prompt_kernel, d_ff=weights[-1].shape[1]),
        grid=(b, t // tm),
        in_specs=[tok(dm), tok(w), tok(w), halo] + [_layer_spec(a, layer) for a in weights],
        out_specs=tok(dm),
        out_shape=jax.ShapeDtypeStruct(h.shape, F32),
        scratch_shapes=[pltpu.VMEM((tm + POOL_HALO, w), F32)],
        compiler_params=pltpu.CompilerParams(
            dimension_semantics=("parallel", "parallel"), vmem_limit_bytes=VMEM_LIMIT_BYTES),
        name="merge_out_prompt",
    )(h, o, p, p, *weights)


def _merge_out_sample_kernel(h_ref, o_ref, d_ref, *rest, d_ff):
    *param_refs, y_ref = rest
    gd = d_ref.shape[1] // len(POOL_WINDOWS)
    diffs = [d_ref[:, pl.ds(g * gd, gd)] for g in range(len(POOL_WINDOWS))]
    _merge_out_body(h_ref, o_ref, diffs, *param_refs, y_ref, d_ff)


def _merge_out_sample(h, o, d, weights, *, layer):
    whole = lambda a: pl.BlockSpec(a.shape, lambda i: (0, 0))
    return pl.pallas_call(
        functools.partial(_merge_out_sample_kernel, d_ff=weights[-1].shape[1]),
        grid=(1,),
        in_specs=[whole(h), whole(o), whole(d)] + [_layer_spec(a, layer) for a in weights],
        out_specs=whole(h),
        out_shape=jax.ShapeDtypeStruct(h.shape, F32),
        compiler_params=pltpu.CompilerParams(
            dimension_semantics=("arbitrary",), vmem_limit_bytes=VMEM_LIMIT_BYTES),
        name="merge_out_sample",
    )(h, o, d, *weights)


TM_PROMPT = 512
TQ = 512
TK = 128
QC = 256
SEQS_PER_STEP = 2


def kernel(x_prompt, x_sample, cache_k, cache_v, state_pool, page_table, rel_bias, ffn1_norm_pre, ffn1_norm_post, ffn1_w_gate, ffn1_w_up, ffn1_w_down, mix_norm_pre, mix_norm_post, w_in, lambda_q1, lambda_k1, lambda_q2, lambda_k2, attn_subln, pool_w, pool_scale, w_branch_a, w_branch_b, w_out, ffn2_norm_pre, ffn2_norm_post, ffn2_w_gate, ffn2_w_up, ffn2_w_down):
    depth = w_in.shape[0]
    bsz, seq, d_model = x_prompt.shape
    n_seq = x_sample.shape[0]
    n_pages = page_table.shape[1]
    past_len = n_pages * PAGE_SIZE
    n_phys = cache_k.shape[1]
    w = ATTN_W
    qkvp_w = 4 * w

    wg1, wu1 = ffn1_w_gate.astype(BF16), ffn1_w_up.astype(BF16)
    wg2, wu2 = ffn2_w_gate.astype(BF16), ffn2_w_up.astype(BF16)
    wd1 = ffn1_w_down.astype(BF16)
    wd2 = ffn2_w_down.astype(BF16)
    w_qkvp = w_in[:, :, :qkvp_w].astype(BF16)
    w_gate = w_in[:, :, qkvp_w:].astype(BF16)
    wa = w_branch_a.astype(BF16)
    wb = w_branch_b.astype(BF16)
    wo = w_out.astype(BF16)
    pw = pool_w.astype(BF16)
    row = lambda a: a[:, None, :]
    lamv = jnp.stack([lambda_q1, lambda_k1, lambda_q2, lambda_k2], axis=1)
    in_w = (row(ffn1_norm_pre), row(ffn1_norm_post), wg1, wu1, wd1, row(mix_norm_pre), w_qkvp)
    out_w = (row(mix_norm_pre), w_gate, pw, row(pool_scale), wa, wb, wo,
             row(mix_norm_post), row(ffn2_norm_pre), row(ffn2_norm_post), wg2, wu2, wd2)
    subln_col = attn_subln[:, :, None]
    subln_row = attn_subln[:, None, :]

    bias_tab = _bias_by_distance(rel_bias, max(TQ + TK, past_len + 1))
    bias_tiles = _prompt_bias_tiles(bias_tab[:, :TQ + TK], TQ, TK)
    bias_past, bias_self = _sample_bias_tables(bias_tab, n_pages)
    ck = cache_k.reshape(depth, n_phys, PAGE_SIZE * N_HEADS, QK_DIM)
    cv = cache_v.reshape(depth, n_phys, PAGE_SIZE * N_HEADS, V_DIM)

    xp = x_prompt
    xs = x_sample.reshape(n_seq, d_model)
    k_prompt = jnp.zeros((depth, bsz, seq, N_HEADS, QK_DIM), F32)
    v_prompt = jnp.zeros((depth, bsz, seq, N_HEADS, V_DIM), F32)
    outs = [[] for _ in range(4)]
    for l in range(depth):
        lam_init = 0.8 - 0.6 * math.exp(-0.3 * l)
        h, p, qb, kb, vb, k_prompt, v_prompt = _ffn_in_prompt(
            xp, in_w, k_prompt, v_prompt, layer=l, tm=TM_PROMPT)
        o = _attn_prompt(qb, kb, vb, bias_tiles, lamv, subln_col,
                         layer=l, tq=TQ, tk=TK, qc=QC, lam_init=lam_init)
        xp = _merge_out_prompt(h, o, p, out_w, layer=l, tm=TM_PROMPT)
        outs[0].append(p[:, seq - POOL_STATE:, :])

        hs, qs, ks, vs, ps = _ffn_in_sample(xs, in_w, layer=l)
        by_head = lambda a: a.reshape(n_seq, N_HEADS, QK_DIM)
        o_s = _attn_sample(page_table, by_head(qs), by_head(ks), by_head(vs), ck, cv,
                           bias_past, bias_self, lamv, subln_row, layer=l, lam_init=lam_init)
        d_s, new_state = _pool_sample(state_pool[l].reshape(n_seq, POOL_STATE * w), ps,
                                      past_len=past_len)
        xs = _merge_out_sample(hs, o_s.reshape(n_seq, w), d_s, out_w, layer=l)
        outs[1].append(ks.reshape(n_seq, 1, N_HEADS, QK_DIM))
        outs[2].append(vs.reshape(n_seq, 1, N_HEADS, V_DIM))
        outs[3].append(new_state.reshape(n_seq, POOL_STATE, w))

    stacked = [jnp.stack(o) for o in outs]
    return (xp, xs.reshape(n_seq, 1, d_model), k_prompt, v_prompt, stacked[0],
            stacked[1], stacked[2], stacked[3])
```

```python
import functools
import math

import jax
import jax.numpy as jnp
from jax import lax
from jax.experimental import pallas as pl
from jax.experimental.pallas import tpu as pltpu

F32 = jnp.float32
BF16 = jnp.bfloat16

N_HEADS = 4
HEAD_DIM = 64
QK_DIM = 2 * HEAD_DIM
V_DIM = 2 * HEAD_DIM
ATTN_W = N_HEADS * V_DIM
POOL_WINDOWS = (2, 4, 8, 16)
POOL_STATE = max(POOL_WINDOWS) - 1
N_BUCKETS = 32
MAX_DISTANCE = 128
PAGE_SIZE = 128
NORM_EPS = 1e-6
NEG_INF = -1e30
SCALE = HEAD_DIM ** -0.5
LOG2E = math.log2(math.e)

LANES = 128
SUBLANES = 8
VMEM_LIMIT_BYTES = 56 * 1024 * 1024
MXU_TILE = 256
FF_CHUNK = 6 * MXU_TILE
PAGE_GROUP = LANES // (4 * N_HEADS)


def _rms(x, g):
    ms = jnp.mean(x * x, axis=-1, keepdims=True)
    return x * lax.rsqrt(ms + NORM_EPS) * g


def _layer_spec(stacked, layer):
    zeros = (0,) * (stacked.ndim - 1)
    return pl.BlockSpec((None,) + stacked.shape[1:], lambda *_: (layer,) + zeros,
                        pipeline_mode=pl.Buffered(1))


def _swiglu(u, wg_ref, wu_ref, wd_ref, d_ff):
    f = None
    for lo in range(0, d_ff, FF_CHUNK):
        hi = min(lo + FF_CHUNK, d_ff)
        g = jnp.dot(u, wg_ref[:, lo:hi], preferred_element_type=F32)
        up = jnp.dot(u, wu_ref[:, lo:hi], preferred_element_type=F32)
        act = (g * jax.nn.sigmoid(g) * up).astype(BF16)
        part = jnp.dot(act, wd_ref[lo:hi, :], preferred_element_type=F32)
        f = part if f is None else f + part
    return f


def _ffn_in_body(x_ref, pre_ref, post_ref, wg_ref, wu_ref, wd_ref, mpre_ref, win_ref, d_ff):
    x = x_ref[...]
    f = _swiglu(_rms(x, pre_ref[...]).astype(BF16), wg_ref, wu_ref, wd_ref, d_ff)
    h = x + 0.5 * _rms(f, post_ref[...])
    u = _rms(h, mpre_ref[...]).astype(BF16)
    proj = jnp.dot(u, win_ref[...], preferred_element_type=F32)
    w = ATTN_W
    return h, proj[:, :w], proj[:, w:2 * w], proj[:, 2 * w:3 * w], proj[:, 3 * w:]


def _ffn_in_prompt_kernel(x_ref, *rest, d_ff):
    *param_refs, kbuf_ref, vbuf_ref, h_ref, p_ref, qb_ref, kb_ref, vb_ref, k5_ref, v5_ref = rest
    del kbuf_ref, vbuf_ref
    h, q, k, v, p = _ffn_in_body(x_ref, *param_refs, d_ff)
    h_ref[...] = h
    p_ref[...] = p
    qb_ref[...] = (q * (SCALE * LOG2E)).astype(BF16)
    kb_ref[...] = k.astype(BF16)
    vb_ref[...] = v.astype(BF16)
    for hd in range(N_HEADS):
        k5_ref[:, hd, :] = k[:, hd * QK_DIM:(hd + 1) * QK_DIM]
        v5_ref[:, hd, :] = v[:, hd * V_DIM:(hd + 1) * V_DIM]


def _ffn_in_prompt(x, weights, kbuf, vbuf, *, layer, tm):
    b, t, d = x.shape
    d_ff = weights[4].shape[1]
    w = ATTN_W
    tok = lambda width: pl.BlockSpec((None, tm, width), lambda bi, i: (bi, i, 0))
    kv5 = pl.BlockSpec((None, None, tm, N_HEADS, QK_DIM), lambda bi, i: (layer, bi, i, 0, 0))
    any_spec = pl.BlockSpec(memory_space=pl.ANY)
    n_in = 1 + len(weights)
    return pl.pallas_call(
        functools.partial(_ffn_in_prompt_kernel, d_ff=d_ff),
        grid=(b, t // tm),
        in_specs=[tok(d)] + [_layer_spec(a, layer) for a in weights] + [any_spec, any_spec],
        out_specs=[tok(d), tok(w), tok(w), tok(w), tok(w), kv5, kv5],
        out_shape=[jax.ShapeDtypeStruct((b, t, d), F32),
                   jax.ShapeDtypeStruct((b, t, w), F32),
                   jax.ShapeDtypeStruct((b, t, w), BF16),
                   jax.ShapeDtypeStruct((b, t, w), BF16),
                   jax.ShapeDtypeStruct((b, t, w), BF16),
                   jax.ShapeDtypeStruct(kbuf.shape, F32),
                   jax.ShapeDtypeStruct(vbuf.shape, F32)],
        input_output_aliases={n_in: 5, n_in + 1: 6},
        compiler_params=pltpu.CompilerParams(
            dimension_semantics=("parallel", "parallel"), vmem_limit_bytes=VMEM_LIMIT_BYTES),
        name="ffn_in_prompt",
    )(x, *weights, kbuf, vbuf)


def _ffn_in_sample_kernel(x_ref, *rest, d_ff):
    *param_refs, h_ref, q_ref, k_ref, v_ref, p_ref = rest
    h, q, k, v, p = _ffn_in_body(x_ref, *param_refs, d_ff)
    h_ref[...] = h
    q_ref[...] = q * SCALE
    k_ref[...] = k
    v_ref[...] = v
    p_ref[...] = p


def _ffn_in_sample(x, weights, *, layer):
    n, d = x.shape
    d_ff = weights[4].shape[1]
    whole = lambda width: pl.BlockSpec((n, width), lambda i: (0, 0))
    return pl.pallas_call(
        functools.partial(_ffn_in_sample_kernel, d_ff=d_ff),
        grid=(1,),
        in_specs=[whole(d)] + [_layer_spec(a, layer) for a in weights],
        out_specs=[whole(d)] + [whole(ATTN_W)] * 4,
        out_shape=[jax.ShapeDtypeStruct((n, d), F32)] + [jax.ShapeDtypeStruct((n, ATTN_W), F32)] * 4,
        compiler_params=pltpu.CompilerParams(
            dimension_semantics=("arbitrary",), vmem_limit_bytes=VMEM_LIMIT_BYTES),
        name="ffn_in_sample",
    )(x, *weights)


def _lambda_value(lamv, lam_init):
    s1 = jnp.sum(lamv[0:1] * lamv[1:2], axis=1, keepdims=True)
    s2 = jnp.sum(lamv[2:3] * lamv[3:4], axis=1, keepdims=True)
    return jnp.exp(s1) - jnp.exp(s2) + lam_init


def _bias_by_distance(rel_bias, n):
    dist = jnp.arange(n, dtype=jnp.int32)
    max_exact = N_BUCKETS // 2
    nf = jnp.maximum(dist, 1).astype(F32)
    large = max_exact + (jnp.log(nf / max_exact) / math.log(MAX_DISTANCE / max_exact)
                         * (N_BUCKETS - max_exact)).astype(jnp.int32)
    large = jnp.minimum(large, N_BUCKETS - 1)
    bucket = jnp.where(dist < max_exact, dist, large)
    return rel_bias[bucket].T.astype(F32)


def _attn_prompt_kernel(q_ref, k_ref, v_ref, bias_ref, lamv_ref, subln_ref, o_ref,
                        vt_ref, q2_ref, m_ref, l_ref, acc_ref, *, tq, tk, qc, lam_init):
    qi = pl.program_id(2)
    ratio = tq // tk
    n_special = bias_ref.shape[0]
    n_chunks = 2 * tq // qc

    @pl.when(qi == 0)
    def _():
        def transpose_tile(j, carry):
            start = pl.multiple_of(j * tk, tk)
            vt_ref[j] = v_ref[pl.ds(start, tk), :].T
            return carry
        lax.fori_loop(0, vt_ref.shape[0], transpose_tile, 0)

    qt = q_ref[...].T
    row = lax.broadcasted_iota(jnp.int32, qt.shape, 0)
    zero = jnp.zeros_like(qt)
    q2_ref[:, pl.ds(0, tq)] = jnp.where(row < HEAD_DIM, qt, zero)
    q2_ref[:, pl.ds(tq, tq)] = jnp.where(row >= HEAD_DIM, qt, zero)
    m_ref[...] = jnp.full(m_ref.shape, NEG_INF, F32)
    l_ref[...] = jnp.zeros(l_ref.shape, F32)
    acc_ref[...] = jnp.zeros(acc_ref.shape, F32)

    def step(kj, special):
        start = pl.multiple_of(kj * tk, tk)
        k = k_ref[pl.ds(start, tk), :]
        vt = vt_ref[kj]
        for c in range(n_chunks):
            q_lo = (c * qc) % tq
            if special is not None and q_lo + qc - 1 < (special - 1) * tk:
                continue
            cols = pl.ds(c * qc, qc)
            s = jnp.dot(k, q2_ref[:, cols], preferred_element_type=F32)
            if special is not None:
                s = s + bias_ref[special, :, pl.ds(q_lo, qc)]
            m_old = m_ref[:, cols]
            m_new = jnp.maximum(m_old, jnp.max(s, axis=0, keepdims=True))
            alpha = jnp.exp2(m_old - m_new)
            p = jnp.exp2(s - m_new)
            l_ref[:, cols] = alpha * l_ref[:, cols] + jnp.sum(p, axis=0, keepdims=True)
            pv = jnp.dot(vt, p.astype(BF16), preferred_element_type=F32)
            acc_ref[:, cols] = alpha * acc_ref[:, cols] + pv
            m_ref[:, cols] = m_new

    def run_tiles(first, specials):
        for t, special in enumerate(specials):
            step(first + t, special)

    n_far_blocks = jnp.maximum(qi - 1, 0)

    def far_group(g, carry):
        run_tiles(g * 2 * ratio, [None] * (2 * ratio))
        return carry

    lax.fori_loop(0, n_far_blocks // 2, far_group, 0)

    @pl.when(n_far_blocks % 2 == 1)
    def _():
        run_tiles((n_far_blocks - 1) * ratio, [None] * ratio)

    @pl.when(qi > 0)
    def _():
        run_tiles((qi - 1) * ratio, [None] * (ratio - 1) + list(range(n_special)))

    @pl.when(qi == 0)
    def _():
        run_tiles(0, list(range(1, n_special)))

    l = l_ref[...]
    acc = acc_ref[...]
    lam = _lambda_value(lamv_ref[...], lam_init)
    o = acc[:, :tq] / l[:, :tq] - lam * (acc[:, tq:] / l[:, tq:])
    ms = jnp.mean(o * o, axis=0, keepdims=True)
    y = o * lax.rsqrt(ms + NORM_EPS) * subln_ref[...] * (1.0 - lam_init)
    o_ref[...] = y.T.astype(BF16)


def _prompt_bias_tiles(bias_tab, tq, tk):
    n_special = tq // tk + 1
    rows = n_special * tk
    period = rows + tq
    x = jnp.arange(period, dtype=jnp.int32)
    dist = jnp.where(x < tq, x, x - period) + tk
    shifted = (bias_tab - bias_tab[:, -1:]) * LOG2E
    vals = jnp.where(dist < 0, NEG_INF, shifted[:, jnp.clip(dist, 0, bias_tab.shape[1] - 1)])
    flat = jnp.tile(vals, (1, rows))[:, :rows * (period - 1)]
    toeplitz = flat.reshape(-1, rows, period - 1)[:, :, :tq]
    return toeplitz.reshape(-1, n_special, tk, tq).astype(F32)


def _attn_prompt(qb, kb, vb, bias_tiles, lamv, subln_col, *, layer, tq, tk, qc, lam_init):
    b, t, w = qb.shape
    nk = t // tk
    n_special = bias_tiles.shape[1]
    grid = (b, N_HEADS, t // tq)
    return pl.pallas_call(
        functools.partial(_attn_prompt_kernel, tq=tq, tk=tk, qc=qc, lam_init=lam_init),
        grid=grid,
        in_specs=[
            pl.BlockSpec((None, tq, QK_DIM), lambda bi, h, qi: (bi, qi, h)),
            pl.BlockSpec((None, t, QK_DIM), lambda bi, h, qi: (bi, 0, h)),
            pl.BlockSpec((None, t, V_DIM), lambda bi, h, qi: (bi, 0, h)),
            pl.BlockSpec((None, n_special, tk, tq), lambda bi, h, qi: (h, 0, 0, 0)),
            pl.BlockSpec((None,) + lamv.shape[1:], lambda bi, h, qi: (layer, 0, 0)),
            pl.BlockSpec((None,) + subln_col.shape[1:], lambda bi, h, qi: (layer, 0, 0)),
        ],
        out_specs=pl.BlockSpec((None, tq, V_DIM), lambda bi, h, qi: (bi, qi, h)),
        out_shape=jax.ShapeDtypeStruct((b, t, w), BF16),
        scratch_shapes=[pltpu.VMEM((nk, V_DIM, tk), BF16),
                        pltpu.VMEM((QK_DIM, 2 * tq), BF16),
                        pltpu.VMEM((1, 2 * tq), F32),
                        pltpu.VMEM((1, 2 * tq), F32),
                        pltpu.VMEM((V_DIM, 2 * tq), F32)],
        compiler_params=pltpu.CompilerParams(
            dimension_semantics=("parallel", "parallel", "arbitrary"),
            vmem_limit_bytes=VMEM_LIMIT_BYTES),
        name="attn_prompt",
    )(qb, kb, vb, bias_tiles, lamv, subln_col)


def _attn_sample_kernel(pt_ref, q_ref, kn_ref, vn_ref, bias_ref, bias_self_ref, lamv_ref,
                        subln_ref, *rest, n_pages, lam_init):
    del pt_ref
    kp_refs = rest[:n_pages]
    vp_refs = rest[n_pages:2 * n_pages]
    o_ref, w_ref, ws_ref = rest[2 * n_pages:]
    qh, kn, vn = q_ref[...], kn_ref[...], vn_ref[...]

    @pl.when(pl.program_id(0) == 0)
    def _():
        w_ref[...] = jnp.zeros(w_ref.shape, F32)
        ws_ref[...] = jnp.zeros(ws_ref.shape, F32)

    nt = (((1,), (1,)), ((), ()))
    half = PAGE_SIZE * N_HEADS // 2
    n_col = 2 * N_HEADS
    page_cols = 2 * n_col
    n_groups = len(kp_refs) // PAGE_GROUP

    lane = lax.broadcasted_iota(jnp.int32, qh.shape, 1)
    zero = jnp.zeros_like(qh)
    q8 = jnp.concatenate([jnp.where(lane < HEAD_DIM, qh, zero),
                          jnp.where(lane >= HEAD_DIM, qh, zero)], axis=0)
    z8 = jnp.zeros_like(q8)
    rt = jnp.concatenate([jnp.concatenate([q8, z8], axis=1),
                          jnp.concatenate([z8, q8], axis=1)], axis=0)
    wide = (PAGE_GROUP * page_cols, PAGE_GROUP * 2 * LANES)
    own = (lax.broadcasted_iota(jnp.int32, wide, 0) // page_cols
           == lax.broadcasted_iota(jnp.int32, wide, 1) // (2 * LANES))
    rt_cat = jnp.where(own, jnp.tile(rt, (PAGE_GROUP, PAGE_GROUP)), 0.0)

    def paired(ref):
        return jnp.concatenate([ref[pl.ds(0, half), :], ref[pl.ds(half, half), :]], axis=1)

    def spread(x16, fill):
        return jnp.concatenate(
            [x16, jnp.full((1, (PAGE_GROUP - 1) * page_cols), fill, F32)], axis=1)

    def over_columns(x, op):
        shift = n_col
        while shift < x.shape[1]:
            x = op(x, pltpu.roll(x, shift, 1))
            shift *= 2
        return x

    esh = (PAGE_GROUP * page_cols, 2 * LANES)
    erow = lax.broadcasted_iota(jnp.int32, esh, 0)
    first_half_lane = lax.broadcasted_iota(jnp.int32, esh, 1) < LANES
    first_half_col = erow % page_cols < n_col

    s_groups = []
    acc = jnp.zeros((half, V_DIM), F32)
    for grp in range(n_groups):
        pages = kp_refs[grp * PAGE_GROUP:(grp + 1) * PAGE_GROUP]
        lhs = jnp.concatenate([paired(kp) for kp in pages], axis=1)
        s_groups.append(lax.dot_general(lhs, rt_cat, nt, preferred_element_type=F32)
                        + bias_ref[grp])
        w_prev = w_ref[grp]
        for g in range(PAGE_GROUP):
            expand = ((erow // page_cols == g) & (first_half_col == first_half_lane)).astype(F32)
            wb = jnp.dot(w_prev, expand, preferred_element_type=F32)
            vp = vp_refs[grp * PAGE_GROUP + g]
            acc = (acc + wb[:, :LANES] * vp[pl.ds(0, half), :]
                   + wb[:, LANES:] * vp[pl.ds(half, half), :])

    acc8 = jnp.sum(acc.reshape(half // SUBLANES, SUBLANES, V_DIM), axis=0)
    expand_self = (first_half_col == first_half_lane)[:page_cols].astype(F32)
    wb_self = jnp.dot(ws_ref[...], expand_self, preferred_element_type=F32)
    o = acc8[:N_HEADS] + acc8[N_HEADS:] + wb_self[:N_HEADS, :LANES] * vn
    ms = jnp.mean(o * o, axis=1, keepdims=True)
    o_ref[...] = o * lax.rsqrt(ms + NORM_EPS) * subln_ref[...] * (1.0 - lam_init)

    kn8 = jnp.concatenate([kn, kn], axis=0)
    s_self = lax.dot_general(jnp.concatenate([kn8, jnp.zeros_like(kn8)], axis=1), rt, nt,
                             preferred_element_type=F32) + bias_self_ref[...]
    m_in = spread(jnp.max(s_self, axis=0, keepdims=True), NEG_INF)
    for s in s_groups:
        m_in = jnp.maximum(m_in, jnp.max(s, axis=0, keepdims=True))
    m = over_columns(m_in, jnp.maximum)
    p_groups = [jnp.exp(s - m) for s in s_groups]
    p_self = jnp.exp(s_self - m[:, :page_cols])
    l_in = spread(jnp.sum(p_self, axis=0, keepdims=True), 0.0)
    for p in p_groups:
        l_in = l_in + jnp.sum(p, axis=0, keepdims=True)
    l = over_columns(l_in, jnp.add)
    lam = _lambda_value(lamv_ref[...], lam_init)
    col = lax.broadcasted_iota(jnp.int32, l.shape, 1) % n_col
    coef = jnp.where(col < N_HEADS, 1.0, -lam) / l
    for grp in range(n_groups):
        w_ref[grp] = p_groups[grp] * coef
    ws_ref[...] = p_self * coef[:, :page_cols]


def _sample_bias_tables(bias_tab, n_pages):
    past_len = n_pages * PAGE_SIZE
    half_tok = PAGE_SIZE // 2
    rev = bias_tab[:, 1:past_len + 1][:, ::-1]
    vals = rev.reshape(N_HEADS, n_pages, 2, half_tok).transpose(1, 3, 0, 2)
    same_head = jnp.eye(N_HEADS, dtype=bool)
    full = jnp.where(same_head[None, None, :, None, None, :],
                     vals[:, :, :, :, None, None], NEG_INF)
    full = jnp.broadcast_to(full, (n_pages, half_tok, N_HEADS, 2, 2, N_HEADS))
    rows = half_tok * N_HEADS
    bias_past = full.reshape(n_pages // PAGE_GROUP, PAGE_GROUP, rows, 4 * N_HEADS)
    bias_past = bias_past.transpose(0, 2, 1, 3).reshape(n_pages // PAGE_GROUP, rows, LANES)
    self_vals = jnp.where(same_head[:, None, :], bias_tab[:, 0][:, None, None], NEG_INF)
    self_vals = jnp.broadcast_to(self_vals, (N_HEADS, 2, N_HEADS)).reshape(N_HEADS, 2 * N_HEADS)
    top = jnp.concatenate([self_vals, jnp.full_like(self_vals, NEG_INF)], axis=1)
    bias_self = jnp.concatenate([top, jnp.full_like(top, NEG_INF)], axis=0)
    return bias_past.astype(F32), bias_self.astype(F32)


def _attn_sample(page_table, q, k_new, v_new, cache_k, cache_v, bias_past, bias_self, lamv,
                 subln_row, *, layer, lam_init):
    n_seq, n_pages = page_table.shape
    scored = lambda s: jnp.minimum(s, n_seq - 1)
    finished = lambda s: jnp.maximum(s - 1, 0)
    head_spec = lambda seq_of: pl.BlockSpec((None, N_HEADS, QK_DIM),
                                            lambda s, pt: (seq_of(s), 0, 0))
    const2 = lambda a: pl.BlockSpec(a.shape, lambda s, pt: (0,) * a.ndim)
    layered = lambda a: pl.BlockSpec((None,) + a.shape[1:], lambda s, pt: (layer, 0, 0))

    def page_spec(seq_of, j):
        return pl.BlockSpec((None, None, PAGE_SIZE * N_HEADS, QK_DIM),
                            lambda s, pt: (layer, pt[seq_of(s) * n_pages + j], 0, 0))

    grid_spec = pltpu.PrefetchScalarGridSpec(
        num_scalar_prefetch=1,
        grid=(n_seq + 1,),
        in_specs=[head_spec(scored), head_spec(scored), head_spec(finished), const2(bias_past),
                  const2(bias_self), layered(lamv), layered(subln_row)]
                 + [page_spec(scored, j) for j in range(n_pages)]
                 + [page_spec(finished, j) for j in range(n_pages)],
        out_specs=head_spec(finished),
        scratch_shapes=[pltpu.VMEM(bias_past.shape, F32),
                        pltpu.VMEM(bias_self.shape, F32)],
    )
    return pl.pallas_call(
        functools.partial(_attn_sample_kernel, n_pages=n_pages, lam_init=lam_init),
        grid_spec=grid_spec,
        out_shape=jax.ShapeDtypeStruct(q.shape, F32),
        compiler_params=pltpu.CompilerParams(
            dimension_semantics=("arbitrary",), vmem_limit_bytes=VMEM_LIMIT_BYTES),
        name="attn_sample",
    )(page_table.reshape(-1), q, k_new, v_new, bias_past, bias_self, lamv, subln_row,
      *([cache_k] * n_pages), *([cache_v] * n_pages))


POOL_HALO = POOL_STATE + 1


def _pool_diff_prompt(p_ref, halo_ref, ext_ref, tile_index):
    tm = p_ref.shape[0]
    p = p_ref[...]
    ext_ref[pl.ds(0, POOL_HALO), :] = jnp.where(tile_index > 0, halo_ref[...], 0.0)
    ext_ref[pl.ds(POOL_HALO, tm), :] = p
    gd = p.shape[1] // len(POOL_WINDOWS)
    pos1 = tile_index * tm + lax.broadcasted_iota(jnp.int32, (tm, gd), 0) + 1
    diffs = []
    for g, w in enumerate(POOL_WINDOWS):
        lanes = pl.ds(g * gd, gd)
        ws = ext_ref[pl.ds(POOL_HALO, tm), lanes]
        for j in range(1, w):
            ws = ws + ext_ref[pl.ds(POOL_HALO - j, tm), lanes]
        cnt = jnp.minimum(pos1, w).astype(F32)
        diffs.append(ws / cnt - p[:, g * gd:(g + 1) * gd])
    return diffs


def _pool_sample_kernel(state_ref, p_ref, d_ref, new_ref, *, past_len):
    w_tot = p_ref.shape[1]
    gd = w_tot // len(POOL_WINDOWS)
    p = p_ref[...]
    for g, w in enumerate(POOL_WINDOWS):
        ws = p[:, g * gd:(g + 1) * gd]
        for j in range(1, w):
            row = POOL_STATE - j
            ws = ws + state_ref[:, pl.ds(row * w_tot + g * gd, gd)]
        cnt = float(min(past_len + 1, w))
        d_ref[:, pl.ds(g * gd, gd)] = ws / cnt - p[:, g * gd:(g + 1) * gd]
    keep = (POOL_STATE - 1) * w_tot
    new_ref[:, pl.ds(0, keep)] = state_ref[:, pl.ds(w_tot, keep)]
    new_ref[:, pl.ds(keep, w_tot)] = p


def _pool_sample(state, p, *, past_len):
    return pl.pallas_call(
        functools.partial(_pool_sample_kernel, past_len=past_len),
        out_shape=[jax.ShapeDtypeStruct(p.shape, F32), jax.ShapeDtypeStruct(state.shape, F32)],
        name="pool_sample",
    )(state, p)


def _merge_out_body(h_ref, o_ref, diffs, mpre_ref, wgate_ref, poolw_ref, pscale_ref, wa_ref,
                    wb_ref, wout_ref, mpost_ref, pre2_ref, post2_ref, wg_ref, wu_ref, wd_ref, y_ref,
                    d_ff):
    h = h_ref[...]
    d_model = h.shape[1]
    u = _rms(h, mpre_ref[...]).astype(BF16)
    gates = jnp.dot(u, wgate_ref[...], preferred_element_type=F32)
    pooled = jnp.concatenate(
        [jnp.dot(d.astype(BF16), poolw_ref[g], preferred_element_type=F32)
         for g, d in enumerate(diffs)], axis=1) * pscale_ref[...]
    branch_a = jnp.dot(o_ref[...].astype(BF16), wa_ref[...], preferred_element_type=F32)
    branch_b = jnp.dot(pooled.astype(BF16), wb_ref[...], preferred_element_type=F32)
    merged = (jax.nn.sigmoid(gates[:, :d_model]) * branch_a
              + jax.nn.sigmoid(gates[:, d_model:]) * branch_b)
    mixed = jnp.dot(merged.astype(BF16), wout_ref[...], preferred_element_type=F32)
    h2 = h + _rms(mixed, mpost_ref[...])
    f = _swiglu(_rms(h2, pre2_ref[...]).astype(BF16), wg_ref, wu_ref, wd_ref, d_ff)
    y_ref[...] = h2 + 0.5 * _rms(f, post2_ref[...])


def _merge_out_prompt_kernel(h_ref, o_ref, p_ref, halo_ref, *rest, d_ff):
    *param_refs, y_ref, ext_ref = rest
    diffs = _pool_diff_prompt(p_ref, halo_ref, ext_ref, pl.program_id(1))
    _merge_out_body(h_ref, o_ref, diffs, *param_refs, y_ref, d_ff)


def _merge_out_prompt(h, o, p, weights, *, layer, tm):
    b, t, dm = h.shape
    w = p.shape[-1]
    tok = lambda width: pl.BlockSpec((None, tm, width), lambda bi, i: (bi, i, 0))
    per = tm // POOL_HALO
    halo = pl.BlockSpec((None, POOL_HALO, w), lambda bi, i: (bi, jnp.maximum(i * per - 1, 0), 0))
    return pl.pallas_call(
        functools.partial(_merge_out_prompt_kernel, d_ff=weights[-1].shape[1]),
        grid=(b, t // tm),
        in_specs=[tok(dm), tok(w), tok(w), halo] + [_layer_spec(a, layer) for a in weights],
        out_specs=tok(dm),
        out_shape=jax.ShapeDtypeStruct(h.shape, F32),
        scratch_shapes=[pltpu.VMEM((tm + POOL_HALO, w), F32)],
        compiler_params=pltpu.CompilerParams(
            dimension_semantics=("parallel", "parallel"), vmem_limit_bytes=VMEM_LIMIT_BYTES),
        name="merge_out_prompt",
    )(h, o, p, p, *weights)


def _merge_out_sample_kernel(h_ref, o_ref, d_ref, *rest, d_ff):
    *param_refs, y_ref = rest
    gd = d_ref.shape[1] // len(POOL_WINDOWS)
    diffs = [d_ref[:, pl.ds(g * gd, gd)] for g in range(len(POOL_WINDOWS))]
    _merge_out_body(h_ref, o_ref, diffs, *param_refs, y_ref, d_ff)


def _merge_out_sample(h, o, d, weights, *, layer):
    whole = lambda a: pl.BlockSpec(a.shape, lambda i: (0, 0))
    return pl.pallas_call(
        functools.partial(_merge_out_sample_kernel, d_ff=weights[-1].shape[1]),
        grid=(1,),
        in_specs=[whole(h), whole(o), whole(d)] + [_layer_spec(a, layer) for a in weights],
        out_specs=whole(h),
        out_shape=jax.ShapeDtypeStruct(h.shape, F32),
        compiler_params=pltpu.CompilerParams(
            dimension_semantics=("arbitrary",), vmem_limit_bytes=VMEM_LIMIT_BYTES),
        name="merge_out_sample",
    )(h, o, d, *weights)


TM_PROMPT = 512
TQ = 512
TK = 128
QC = 256


def kernel(x_prompt, x_sample, cache_k, cache_v, state_pool, page_table, rel_bias, ffn1_norm_pre, ffn1_norm_post, ffn1_w_gate, ffn1_w_up, ffn1_w_down, mix_norm_pre, mix_norm_post, w_in, lambda_q1, lambda_k1, lambda_q2, lambda_k2, attn_subln, pool_w, pool_scale, w_branch_a, w_branch_b, w_out, ffn2_norm_pre, ffn2_norm_post, ffn2_w_gate, ffn2_w_up, ffn2_w_down):
    depth = w_in.shape[0]
    bsz, seq, d_model = x_prompt.shape
    n_seq = x_sample.shape[0]
    n_pages = page_table.shape[1]
    past_len = n_pages * PAGE_SIZE
    n_phys = cache_k.shape[1]
    w = ATTN_W
    qkvp_w = 4 * w

    wg1, wu1 = ffn1_w_gate.astype(BF16), ffn1_w_up.astype(BF16)
    wg2, wu2 = ffn2_w_gate.astype(BF16), ffn2_w_up.astype(BF16)
    wd1 = ffn1_w_down.astype(BF16)
    wd2 = ffn2_w_down.astype(BF16)
    w_qkvp = w_in[:, :, :qkvp_w].astype(BF16)
    w_gate = w_in[:, :, qkvp_w:].astype(BF16)
    wa = w_branch_a.astype(BF16)
    wb = w_branch_b.astype(BF16)
    wo = w_out.astype(BF16)
    pw = pool_w.astype(BF16)
    row = lambda a: a[:, None, :]
    lamv = jnp.stack([lambda_q1, lambda_k1, lambda_q2, lambda_k2], axis=1)
    in_w = (row(ffn1_norm_pre), row(ffn1_norm_post), wg1, wu1, wd1, row(mix_norm_pre), w_qkvp)
    out_w = (row(mix_norm_pre), w_gate, pw, row(pool_scale), wa, wb, wo,
             row(mix_norm_post), row(ffn2_norm_pre), row(ffn2_norm_post), wg2, wu2, wd2)
    subln_col = attn_subln[:, :, None]
    subln_row = attn_subln[:, None, :]

    bias_tab = _bias_by_distance(rel_bias, max(TQ + TK, past_len + 1))
    bias_tiles = _prompt_bias_tiles(bias_tab[:, :TQ + TK], TQ, TK)
    bias_past, bias_self = _sample_bias_tables(bias_tab, n_pages)
    ck = cache_k.reshape(depth, n_phys, PAGE_SIZE * N_HEADS, QK_DIM)
    cv = cache_v.reshape(depth, n_phys, PAGE_SIZE * N_HEADS, V_DIM)

    xp = x_prompt
    xs = x_sample.reshape(n_seq, d_model)
    k_prompt = jnp.zeros((depth, bsz, seq, N_HEADS, QK_DIM), F32)
    v_prompt = jnp.zeros((depth, bsz, seq, N_HEADS, V_DIM), F32)
    outs = [[] for _ in range(4)]
    for l in range(depth):
        lam_init = 0.8 - 0.6 * math.exp(-0.3 * l)
        h, p, qb, kb, vb, k_prompt, v_prompt = _ffn_in_prompt(
            xp, in_w, k_prompt, v_prompt, layer=l, tm=TM_PROMPT)
        o = _attn_prompt(qb, kb, vb, bias_tiles, lamv, subln_col,
                         layer=l, tq=TQ, tk=TK, qc=QC, lam_init=lam_init)
        xp = _merge_out_prompt(h, o, p, out_w, layer=l, tm=TM_PROMPT)
        outs[0].append(p[:, seq - POOL_STATE:, :])

        hs, qs, ks, vs, ps = _ffn_in_sample(xs, in_w, layer=l)
        by_head = lambda a: a.reshape(n_seq, N_HEADS, QK_DIM)
        o_s = _attn_sample(page_table, by_head(qs), by_head(ks), by_head(vs), ck, cv,
                           bias_past, bias_self, lamv, subln_row, layer=l, lam_init=lam_init)
        d_s, new_state = _pool_sample(state_pool[l].reshape(n_seq, POOL_STATE * w), ps,
                                      past_len=past_len)
        xs = _merge_out_sample(hs, o_s.reshape(n_seq, w), d_s, out_w, layer=l)
        outs[1].append(ks.reshape(n_seq, 1, N_HEADS, QK_DIM))
        outs[2].append(vs.reshape(n_seq, 1, N_HEADS, V_DIM))
        outs[3].append(new_state.reshape(n_seq, POOL_STATE, w))

    stacked = [jnp.stack(o) for o in outs]
    return (xp, xs.reshape(n_seq, 1, d_model), k_prompt, v_prompt, stacked[0],
            stacked[1], stacked[2], stacked[3])
```

```python
import functools
import math

import jax
import jax.numpy as jnp
from jax import lax
from jax.experimental import pallas as pl
from jax.experimental.pallas import tpu as pltpu

F32 = jnp.float32
BF16 = jnp.bfloat16

N_HEADS = 4
HEAD_DIM = 64
QK_DIM = 2 * HEAD_DIM
V_DIM = 2 * HEAD_DIM
ATTN_W = N_HEADS * V_DIM
POOL_WINDOWS = (2, 4, 8, 16)
POOL_STATE = max(POOL_WINDOWS) - 1
N_BUCKETS = 32
MAX_DISTANCE = 128
PAGE_SIZE = 128
NORM_EPS = 1e-6
NEG_INF = -1e30
SCALE = HEAD_DIM ** -0.5
LOG2E = math.log2(math.e)

LANES = 128
SUBLANES = 8
VMEM_LIMIT_BYTES = 56 * 1024 * 1024
MXU_TILE = 256
FF_CHUNK = 6 * MXU_TILE
PAGE_GROUP = LANES // (4 * N_HEADS)
PAGE_BUFFERS = 3


def _rms(x, g):
    ms = jnp.mean(x * x, axis=-1, keepdims=True)
    return x * lax.rsqrt(ms + NORM_EPS) * g


def _layer_spec(stacked, layer):
    zeros = (0,) * (stacked.ndim - 1)
    return pl.BlockSpec((None,) + stacked.shape[1:], lambda *_: (layer,) + zeros,
                        pipeline_mode=pl.Buffered(1))


def _swiglu(u, wg_ref, wu_ref, wd_ref, d_ff):
    f = None
    for lo in range(0, d_ff, FF_CHUNK):
        hi = min(lo + FF_CHUNK, d_ff)
        g = jnp.dot(u, wg_ref[:, lo:hi], preferred_element_type=F32)
        up = jnp.dot(u, wu_ref[:, lo:hi], preferred_element_type=F32)
        act = (g * jax.nn.sigmoid(g) * up).astype(BF16)
        part = jnp.dot(act, wd_ref[lo:hi, :], preferred_element_type=F32)
        f = part if f is None else f + part
    return f


def _ffn_in_body(x_ref, pre_ref, post_ref, wg_ref, wu_ref, wd_ref, mpre_ref, win_ref, d_ff):
    x = x_ref[...]
    f = _swiglu(_rms(x, pre_ref[...]).astype(BF16), wg_ref, wu_ref, wd_ref, d_ff)
    h = x + 0.5 * _rms(f, post_ref[...])
    u = _rms(h, mpre_ref[...]).astype(BF16)
    proj = jnp.dot(u, win_ref[...], preferred_element_type=F32)
    w = ATTN_W
    return h, proj[:, :w], proj[:, w:2 * w], proj[:, 2 * w:3 * w], proj[:, 3 * w:]


def _ffn_in_prompt_kernel(x_ref, *rest, d_ff):
    *param_refs, kbuf_ref, vbuf_ref, h_ref, p_ref, qb_ref, kb_ref, vb_ref, k5_ref, v5_ref = rest
    del kbuf_ref, vbuf_ref
    h, q, k, v, p = _ffn_in_body(x_ref, *param_refs, d_ff)
    h_ref[...] = h
    p_ref[...] = p
    qb_ref[...] = (q * (SCALE * LOG2E)).astype(BF16)
    kb_ref[...] = k.astype(BF16)
    vb_ref[...] = v.astype(BF16)
    for hd in range(N_HEADS):
        k5_ref[:, hd, :] = k[:, hd * QK_DIM:(hd + 1) * QK_DIM]
        v5_ref[:, hd, :] = v[:, hd * V_DIM:(hd + 1) * V_DIM]


def _ffn_in_prompt(x, weights, kbuf, vbuf, *, layer, tm):
    b, t, d = x.shape
    d_ff = weights[4].shape[1]
    w = ATTN_W
    tok = lambda width: pl.BlockSpec((None, tm, width), lambda bi, i: (bi, i, 0))
    kv5 = pl.BlockSpec((None, None, tm, N_HEADS, QK_DIM), lambda bi, i: (layer, bi, i, 0, 0))
    any_spec = pl.BlockSpec(memory_space=pl.ANY)
    n_in = 1 + len(weights)
    return pl.pallas_call(
        functools.partial(_ffn_in_prompt_kernel, d_ff=d_ff),
        grid=(b, t // tm),
        in_specs=[tok(d)] + [_layer_spec(a, layer) for a in weights] + [any_spec, any_spec],
        out_specs=[tok(d), tok(w), tok(w), tok(w), tok(w), kv5, kv5],
        out_shape=[jax.ShapeDtypeStruct((b, t, d), F32),
                   jax.ShapeDtypeStruct((b, t, w), F32),
                   jax.ShapeDtypeStruct((b, t, w), BF16),
                   jax.ShapeDtypeStruct((b, t, w), BF16),
                   jax.ShapeDtypeStruct((b, t, w), BF16),
                   jax.ShapeDtypeStruct(kbuf.shape, F32),
                   jax.ShapeDtypeStruct(vbuf.shape, F32)],
        input_output_aliases={n_in: 5, n_in + 1: 6},
        compiler_params=pltpu.CompilerParams(
            dimension_semantics=("parallel", "parallel"), vmem_limit_bytes=VMEM_LIMIT_BYTES),
        name="ffn_in_prompt",
    )(x, *weights, kbuf, vbuf)


def _ffn_in_sample_kernel(x_ref, *rest, d_ff):
    *param_refs, h_ref, q_ref, k_ref, v_ref, p_ref = rest
    h, q, k, v, p = _ffn_in_body(x_ref, *param_refs, d_ff)
    h_ref[...] = h
    q_ref[...] = q * SCALE
    k_ref[...] = k
    v_ref[...] = v
    p_ref[...] = p


def _ffn_in_sample(x, weights, *, layer):
    n, d = x.shape
    d_ff = weights[4].shape[1]
    whole = lambda width: pl.BlockSpec((n, width), lambda i: (0, 0))
    return pl.pallas_call(
        functools.partial(_ffn_in_sample_kernel, d_ff=d_ff),
        grid=(1,),
        in_specs=[whole(d)] + [_layer_spec(a, layer) for a in weights],
        out_specs=[whole(d)] + [whole(ATTN_W)] * 4,
        out_shape=[jax.ShapeDtypeStruct((n, d), F32)] + [jax.ShapeDtypeStruct((n, ATTN_W), F32)] * 4,
        compiler_params=pltpu.CompilerParams(
            dimension_semantics=("arbitrary",), vmem_limit_bytes=VMEM_LIMIT_BYTES),
        name="ffn_in_sample",
    )(x, *weights)


def _lambda_value(lamv, lam_init):
    s1 = jnp.sum(lamv[0:1] * lamv[1:2], axis=1, keepdims=True)
    s2 = jnp.sum(lamv[2:3] * lamv[3:4], axis=1, keepdims=True)
    return jnp.exp(s1) - jnp.exp(s2) + lam_init


def _bias_by_distance(rel_bias, n):
    dist = jnp.arange(n, dtype=jnp.int32)
    max_exact = N_BUCKETS // 2
    nf = jnp.maximum(dist, 1).astype(F32)
    large = max_exact + (jnp.log(nf / max_exact) / math.log(MAX_DISTANCE / max_exact)
                         * (N_BUCKETS - max_exact)).astype(jnp.int32)
    large = jnp.minimum(large, N_BUCKETS - 1)
    bucket = jnp.where(dist < max_exact, dist, large)
    return rel_bias[bucket].T.astype(F32)


def _attn_prompt_kernel(q_ref, k_ref, v_ref, bias_ref, lamv_ref, subln_ref, o_ref,
                        vt_ref, q2_ref, m_ref, l_ref, acc_ref, *, tq, tk, qc, lam_init):
    qi = pl.program_id(2)
    ratio = tq // tk
    n_special = bias_ref.shape[0]
    n_chunks = 2 * tq // qc

    @pl.when(qi == 0)
    def _():
        def transpose_tile(j, carry):
            start = pl.multiple_of(j * tk, tk)
            vt_ref[j] = v_ref[pl.ds(start, tk), :].T
            return carry
        lax.fori_loop(0, vt_ref.shape[0], transpose_tile, 0)

    qt = q_ref[...].T
    row = lax.broadcasted_iota(jnp.int32, qt.shape, 0)
    zero = jnp.zeros_like(qt)
    q2_ref[:, pl.ds(0, tq)] = jnp.where(row < HEAD_DIM, qt, zero)
    q2_ref[:, pl.ds(tq, tq)] = jnp.where(row >= HEAD_DIM, qt, zero)
    m_ref[...] = jnp.full(m_ref.shape, NEG_INF, F32)
    l_ref[...] = jnp.zeros(l_ref.shape, F32)
    acc_ref[...] = jnp.zeros(acc_ref.shape, F32)

    def step(kj, special):
        start = pl.multiple_of(kj * tk, tk)
        k = k_ref[pl.ds(start, tk), :]
        vt = vt_ref[kj]
        for c in range(n_chunks):
            q_lo = (c * qc) % tq
            if special is not None and q_lo + qc - 1 < (special - 1) * tk:
                continue
            cols = pl.ds(c * qc, qc)
            s = jnp.dot(k, q2_ref[:, cols], preferred_element_type=F32)
            if special is not None:
                s = s + bias_ref[special, :, pl.ds(q_lo, qc)]
            m_old = m_ref[:, cols]
            m_new = jnp.maximum(m_old, jnp.max(s, axis=0, keepdims=True))
            alpha = jnp.exp2(m_old - m_new)
            p = jnp.exp2(s - m_new)
            l_ref[:, cols] = alpha * l_ref[:, cols] + jnp.sum(p, axis=0, keepdims=True)
            pv = jnp.dot(vt, p.astype(BF16), preferred_element_type=F32)
            acc_ref[:, cols] = alpha * acc_ref[:, cols] + pv
            m_ref[:, cols] = m_new

    def run_tiles(first, specials):
        for t, special in enumerate(specials):
            step(first + t, special)

    n_far_blocks = jnp.maximum(qi - 1, 0)

    def far_group(g, carry):
        run_tiles(g * 2 * ratio, [None] * (2 * ratio))
        return carry

    lax.fori_loop(0, n_far_blocks // 2, far_group, 0)

    @pl.when(n_far_blocks % 2 == 1)
    def _():
        run_tiles((n_far_blocks - 1) * ratio, [None] * ratio)

    @pl.when(qi > 0)
    def _():
        run_tiles((qi - 1) * ratio, [None] * (ratio - 1) + list(range(n_special)))

    @pl.when(qi == 0)
    def _():
        run_tiles(0, list(range(1, n_special)))

    l = l_ref[...]
    acc = acc_ref[...]
    lam = _lambda_value(lamv_ref[...], lam_init)
    o = acc[:, :tq] / l[:, :tq] - lam * (acc[:, tq:] / l[:, tq:])
    ms = jnp.mean(o * o, axis=0, keepdims=True)
    y = o * lax.rsqrt(ms + NORM_EPS) * subln_ref[...] * (1.0 - lam_init)
    o_ref[...] = y.T.astype(BF16)


def _prompt_bias_tiles(bias_tab, tq, tk):
    n_special = tq // tk + 1
    rows = n_special * tk
    period = rows + tq
    x = jnp.arange(period, dtype=jnp.int32)
    dist = jnp.where(x < tq, x, x - period) + tk
    shifted = (bias_tab - bias_tab[:, -1:]) * LOG2E
    vals = jnp.where(dist < 0, NEG_INF, shifted[:, jnp.clip(dist, 0, bias_tab.shape[1] - 1)])
    flat = jnp.tile(vals, (1, rows))[:, :rows * (period - 1)]
    toeplitz = flat.reshape(-1, rows, period - 1)[:, :, :tq]
    return toeplitz.reshape(-1, n_special, tk, tq).astype(F32)


def _attn_prompt(qb, kb, vb, bias_tiles, lamv, subln_col, *, layer, tq, tk, qc, lam_init):
    b, t, w = qb.shape
    nk = t // tk
    n_special = bias_tiles.shape[1]
    grid = (b, N_HEADS, t // tq)
    return pl.pallas_call(
        functools.partial(_attn_prompt_kernel, tq=tq, tk=tk, qc=qc, lam_init=lam_init),
        grid=grid,
        in_specs=[
            pl.BlockSpec((None, tq, QK_DIM), lambda bi, h, qi: (bi, qi, h)),
            pl.BlockSpec((None, t, QK_DIM), lambda bi, h, qi: (bi, 0, h)),
            pl.BlockSpec((None, t, V_DIM), lambda bi, h, qi: (bi, 0, h)),
            pl.BlockSpec((None, n_special, tk, tq), lambda bi, h, qi: (h, 0, 0, 0)),
            pl.BlockSpec((None,) + lamv.shape[1:], lambda bi, h, qi: (layer, 0, 0)),
            pl.BlockSpec((None,) + subln_col.shape[1:], lambda bi, h, qi: (layer, 0, 0)),
        ],
        out_specs=pl.BlockSpec((None, tq, V_DIM), lambda bi, h, qi: (bi, qi, h)),
        out_shape=jax.ShapeDtypeStruct((b, t, w), BF16),
        scratch_shapes=[pltpu.VMEM((nk, V_DIM, tk), BF16),
                        pltpu.VMEM((QK_DIM, 2 * tq), BF16),
                        pltpu.VMEM((1, 2 * tq), F32),
                        pltpu.VMEM((1, 2 * tq), F32),
                        pltpu.VMEM((V_DIM, 2 * tq), F32)],
        compiler_params=pltpu.CompilerParams(
            dimension_semantics=("parallel", "parallel", "arbitrary"),
            vmem_limit_bytes=VMEM_LIMIT_BYTES),
        name="attn_prompt",
    )(qb, kb, vb, bias_tiles, lamv, subln_col)


def _attn_sample_kernel(pt_ref, q_ref, kn_ref, vn_ref, bias_ref, bias_self_ref, lamv_ref,
                        subln_ref, *rest, n_pages, layer, lam_init):
    ck_ref, cv_ref, o_ref, kbuf_ref, vbuf_ref, sem_ref, w_ref, ws_ref = rest
    step = pl.program_id(0)
    n_seq = pl.num_programs(0) - 1
    qh, kn, vn = q_ref[...], kn_ref[...], vn_ref[...]

    def page_copies(t):
        t = jnp.asarray(t, jnp.int32)
        slot = lax.rem(t, jnp.int32(PAGE_BUFFERS))
        k_seq = jnp.minimum(t, n_seq - 1)
        v_seq = jnp.clip(t - 1, 0, n_seq - 1)
        copies = []
        for j in range(n_pages):
            copies.append(pltpu.make_async_copy(
                ck_ref.at[layer, pt_ref[k_seq * n_pages + j]], kbuf_ref.at[slot, j],
                sem_ref.at[0, slot]))
            copies.append(pltpu.make_async_copy(
                cv_ref.at[layer, pt_ref[v_seq * n_pages + j]], vbuf_ref.at[slot, j],
                sem_ref.at[1, slot]))
        return copies

    @pl.when(step == 0)
    def _():
        w_ref[...] = jnp.zeros(w_ref.shape, F32)
        ws_ref[...] = jnp.zeros(ws_ref.shape, F32)
        for t in range(PAGE_BUFFERS - 1):
            for c in page_copies(t):
                c.start()

    for c in page_copies(step):
        c.wait()
    for c in page_copies(step + PAGE_BUFFERS - 1):
        c.start()
    slot = lax.rem(step, jnp.int32(PAGE_BUFFERS))
    kp_refs = [kbuf_ref.at[slot, j] for j in range(n_pages)]
    vp_refs = [vbuf_ref.at[slot, j] for j in range(n_pages)]

    nt = (((1,), (1,)), ((), ()))
    half = PAGE_SIZE * N_HEADS // 2
    n_col = 2 * N_HEADS
    page_cols = 2 * n_col
    n_groups = n_pages // PAGE_GROUP

    lane = lax.broadcasted_iota(jnp.int32, qh.shape, 1)
    zero = jnp.zeros_like(qh)
    q8 = jnp.concatenate([jnp.where(lane < HEAD_DIM, qh, zero),
                          jnp.where(lane >= HEAD_DIM, qh, zero)], axis=0)
    z8 = jnp.zeros_like(q8)
    rt = jnp.concatenate([jnp.concatenate([q8, z8], axis=1),
                          jnp.concatenate([z8, q8], axis=1)], axis=0)
    wide = (PAGE_GROUP * page_cols, PAGE_GROUP * 2 * LANES)
    own = (lax.broadcasted_iota(jnp.int32, wide, 0) // page_cols
           == lax.broadcasted_iota(jnp.int32, wide, 1) // (2 * LANES))
    rt_cat = jnp.where(own, jnp.tile(rt, (PAGE_GROUP, PAGE_GROUP)), 0.0)

    def paired(ref):
        return jnp.concatenate([ref[pl.ds(0, half), :], ref[pl.ds(half, half), :]], axis=1)

    def spread(x16, fill):
        return jnp.concatenate(
            [x16, jnp.full((1, (PAGE_GROUP - 1) * page_cols), fill, F32)], axis=1)

    def over_columns(x, op):
        shift = n_col
        while shift < x.shape[1]:
            x = op(x, pltpu.roll(x, shift, 1))
            shift *= 2
        return x

    esh = (PAGE_GROUP * page_cols, 2 * LANES)
    erow = lax.broadcasted_iota(jnp.int32, esh, 0)
    first_half_lane = lax.broadcasted_iota(jnp.int32, esh, 1) < LANES
    first_half_col = erow % page_cols < n_col

    s_groups = []
    acc = jnp.zeros((half, V_DIM), F32)
    for grp in range(n_groups):
        pages = kp_refs[grp * PAGE_GROUP:(grp + 1) * PAGE_GROUP]
        lhs = jnp.concatenate([paired(kp) for kp in pages], axis=1)
        s_groups.append(lax.dot_general(lhs, rt_cat, nt, preferred_element_type=F32)
                        + bias_ref[grp])
        w_prev = w_ref[grp]
        for g in range(PAGE_GROUP):
            expand = ((erow // page_cols == g) & (first_half_col == first_half_lane)).astype(F32)
            wb = jnp.dot(w_prev, expand, preferred_element_type=F32)
            vp = vp_refs[grp * PAGE_GROUP + g]
            acc = (acc + wb[:, :LANES] * vp[pl.ds(0, half), :]
                   + wb[:, LANES:] * vp[pl.ds(half, half), :])

    acc8 = jnp.sum(acc.reshape(half // SUBLANES, SUBLANES, V_DIM), axis=0)
    expand_self = (first_half_col == first_half_lane)[:page_cols].astype(F32)
    wb_self = jnp.dot(ws_ref[...], expand_self, preferred_element_type=F32)
    o = acc8[:N_HEADS] + acc8[N_HEADS:] + wb_self[:N_HEADS, :LANES] * vn
    ms = jnp.mean(o * o, axis=1, keepdims=True)
    o_ref[...] = o * lax.rsqrt(ms + NORM_EPS) * subln_ref[...] * (1.0 - lam_init)

    kn8 = jnp.concatenate([kn, kn], axis=0)
    s_self = lax.dot_general(jnp.concatenate([kn8, jnp.zeros_like(kn8)], axis=1), rt, nt,
                             preferred_element_type=F32) + bias_self_ref[...]
    m_in = spread(jnp.max(s_self, axis=0, keepdims=True), NEG_INF)
    for s in s_groups:
        m_in = jnp.maximum(m_in, jnp.max(s, axis=0, keepdims=True))
    m = over_columns(m_in, jnp.maximum)
    p_groups = [jnp.exp(s - m) for s in s_groups]
    p_self = jnp.exp(s_self - m[:, :page_cols])
    l_in = spread(jnp.sum(p_self, axis=0, keepdims=True), 0.0)
    for p in p_groups:
        l_in = l_in + jnp.sum(p, axis=0, keepdims=True)
    l = over_columns(l_in, jnp.add)
    lam = _lambda_value(lamv_ref[...], lam_init)
    col = lax.broadcasted_iota(jnp.int32, l.shape, 1) % n_col
    coef = jnp.where(col < N_HEADS, 1.0, -lam) / l
    for grp in range(n_groups):
        w_ref[grp] = p_groups[grp] * coef
    ws_ref[...] = p_self * coef[:, :page_cols]

    @pl.when(step == n_seq)
    def _():
        for ahead in range(1, PAGE_BUFFERS):
            for c in page_copies(step + ahead):
                c.wait()


def _sample_bias_tables(bias_tab, n_pages):
    past_len = n_pages * PAGE_SIZE
    half_tok = PAGE_SIZE // 2
    rev = bias_tab[:, 1:past_len + 1][:, ::-1]
    vals = rev.reshape(N_HEADS, n_pages, 2, half_tok).transpose(1, 3, 0, 2)
    same_head = jnp.eye(N_HEADS, dtype=bool)
    full = jnp.where(same_head[None, None, :, None, None, :],
                     vals[:, :, :, :, None, None], NEG_INF)
    full = jnp.broadcast_to(full, (n_pages, half_tok, N_HEADS, 2, 2, N_HEADS))
    rows = half_tok * N_HEADS
    bias_past = full.reshape(n_pages // PAGE_GROUP, PAGE_GROUP, rows, 4 * N_HEADS)
    bias_past = bias_past.transpose(0, 2, 1, 3).reshape(n_pages // PAGE_GROUP, rows, LANES)
    self_vals = jnp.where(same_head[:, None, :], bias_tab[:, 0][:, None, None], NEG_INF)
    self_vals = jnp.broadcast_to(self_vals, (N_HEADS, 2, N_HEADS)).reshape(N_HEADS, 2 * N_HEADS)
    top = jnp.concatenate([self_vals, jnp.full_like(self_vals, NEG_INF)], axis=1)
    bias_self = jnp.concatenate([top, jnp.full_like(top, NEG_INF)], axis=0)
    return bias_past.astype(F32), bias_self.astype(F32)


def _attn_sample(page_table, q, k_new, v_new, cache_k, cache_v, bias_past, bias_self, lamv,
                 subln_row, *, layer, lam_init):
    n_seq, n_pages = page_table.shape
    scored = lambda s: jnp.minimum(s, n_seq - 1)
    finished = lambda s: jnp.maximum(s - 1, 0)
    head_spec = lambda seq_of: pl.BlockSpec((None, N_HEADS, QK_DIM),
                                            lambda s, pt: (seq_of(s), 0, 0))
    const2 = lambda a: pl.BlockSpec(a.shape, lambda s, pt: (0,) * a.ndim)
    layered = lambda a: pl.BlockSpec((None,) + a.shape[1:], lambda s, pt: (layer, 0, 0))

    any_spec = pl.BlockSpec(memory_space=pl.ANY)
    page_buf = pltpu.VMEM((PAGE_BUFFERS, n_pages, PAGE_SIZE * N_HEADS, QK_DIM), F32)
    grid_spec = pltpu.PrefetchScalarGridSpec(
        num_scalar_prefetch=1,
        grid=(n_seq + 1,),
        in_specs=[head_spec(scored), head_spec(scored), head_spec(finished), const2(bias_past),
                  const2(bias_self), layered(lamv), layered(subln_row), any_spec, any_spec],
        out_specs=head_spec(finished),
        scratch_shapes=[page_buf, page_buf,
                        pltpu.SemaphoreType.DMA((2, PAGE_BUFFERS)),
                        pltpu.VMEM(bias_past.shape, F32),
                        pltpu.VMEM(bias_self.shape, F32)],
    )
    return pl.pallas_call(
        functools.partial(_attn_sample_kernel, n_pages=n_pages, layer=layer, lam_init=lam_init),
        grid_spec=grid_spec,
        out_shape=jax.ShapeDtypeStruct(q.shape, F32),
        compiler_params=pltpu.CompilerParams(
            dimension_semantics=("arbitrary",), vmem_limit_bytes=VMEM_LIMIT_BYTES),
        name="attn_sample",
    )(page_table.reshape(-1), q, k_new, v_new, bias_past, bias_self, lamv, subln_row,
      cache_k, cache_v)


POOL_HALO = POOL_STATE + 1


def _pool_diff_prompt(p_ref, halo_ref, ext_ref, tile_index):
    tm = p_ref.shape[0]
    p = p_ref[...]
    ext_ref[pl.ds(0, POOL_HALO), :] = jnp.where(tile_index > 0, halo_ref[...], 0.0)
    ext_ref[pl.ds(POOL_HALO, tm), :] = p
    gd = p.shape[1] // len(POOL_WINDOWS)
    pos1 = tile_index * tm + lax.broadcasted_iota(jnp.int32, (tm, gd), 0) + 1
    diffs = []
    for g, w in enumerate(POOL_WINDOWS):
        lanes = pl.ds(g * gd, gd)
        ws = ext_ref[pl.ds(POOL_HALO, tm), lanes]
        for j in range(1, w):
            ws = ws + ext_ref[pl.ds(POOL_HALO - j, tm), lanes]
        cnt = jnp.minimum(pos1, w).astype(F32)
        diffs.append(ws / cnt - p[:, g * gd:(g + 1) * gd])
    return diffs


def _pool_sample_kernel(state_ref, p_ref, d_ref, new_ref, *, past_len):
    w_tot = p_ref.shape[1]
    gd = w_tot // len(POOL_WINDOWS)
    p = p_ref[...]
    for g, w in enumerate(POOL_WINDOWS):
        ws = p[:, g * gd:(g + 1) * gd]
        for j in range(1, w):
            row = POOL_STATE - j
            ws = ws + state_ref[:, pl.ds(row * w_tot + g * gd, gd)]
        cnt = float(min(past_len + 1, w))
        d_ref[:, pl.ds(g * gd, gd)] = ws / cnt - p[:, g * gd:(g + 1) * gd]
    keep = (POOL_STATE - 1) * w_tot
    new_ref[:, pl.ds(0, keep)] = state_ref[:, pl.ds(w_tot, keep)]
    new_ref[:, pl.ds(keep, w_tot)] = p


def _pool_sample(state, p, *, past_len):
    return pl.pallas_call(
        functools.partial(_pool_sample_kernel, past_len=past_len),
        out_shape=[jax.ShapeDtypeStruct(p.shape, F32), jax.ShapeDtypeStruct(state.shape, F32)],
        name="pool_sample",
    )(state, p)


def _merge_out_body(h_ref, o_ref, diffs, mpre_ref, wgate_ref, poolw_ref, pscale_ref, wa_ref,
                    wb_ref, wout_ref, mpost_ref, pre2_ref, post2_ref, wg_ref, wu_ref, wd_ref, y_ref,
                    d_ff):
    h = h_ref[...]
    d_model = h.shape[1]
    u = _rms(h, mpre_ref[...]).astype(BF16)
    gates = jnp.dot(u, wgate_ref[...], preferred_element_type=F32)
    pooled = jnp.concatenate(
        [jnp.dot(d.astype(BF16), poolw_ref[g], preferred_element_type=F32)
         for g, d in enumerate(diffs)], axis=1) * pscale_ref[...]
    branch_a = jnp.dot(o_ref[...].astype(BF16), wa_ref[...], preferred_element_type=F32)
    branch_b = jnp.dot(pooled.astype(BF16), wb_ref[...], preferred_element_type=F32)
    merged = (jax.nn.sigmoid(gates[:, :d_model]) * branch_a
              + jax.nn.sigmoid(gates[:, d_model:]) * branch_b)
    mixed = jnp.dot(merged.astype(BF16), wout_ref[...], preferred_element_type=F32)
    h2 = h + _rms(mixed, mpost_ref[...])
    f = _swiglu(_rms(h2, pre2_ref[...]).astype(BF16), wg_ref, wu_ref, wd_ref, d_ff)
    y_ref[...] = h2 + 0.5 * _rms(f, post2_ref[...])


def _merge_out_prompt_kernel(h_ref, o_ref, p_ref, halo_ref, *rest, d_ff):
    *param_refs, y_ref, ext_ref = rest
    diffs = _pool_diff_prompt(p_ref, halo_ref, ext_ref, pl.program_id(1))
    _merge_out_body(h_ref, o_ref, diffs, *param_refs, y_ref, d_ff)


def _merge_out_prompt(h, o, p, weights, *, layer, tm):
    b, t, dm = h.shape
    w = p.shape[-1]
    tok = lambda width: pl.BlockSpec((None, tm, width), lambda bi, i: (bi, i, 0))
    per = tm // POOL_HALO
    halo = pl.BlockSpec((None, POOL_HALO, w), lambda bi, i: (bi, jnp.maximum(i * per - 1, 0), 0))
    return pl.pallas_call(
        functools.partial(_merge_out_prompt_kernel, d_ff=weights[-1].shape[1]),
        grid=(b, t // tm),
        in_specs=[tok(dm), tok(w), tok(w), halo] + [_layer_spec(a, layer) for a in weights],
        out_specs=tok(dm),
        out_shape=jax.ShapeDtypeStruct(h.shape, F32),
        scratch_shapes=[pltpu.VMEM((tm + POOL_HALO, w), F32)],
        compiler_params=pltpu.CompilerParams(
            dimension_semantics=("parallel", "parallel"), vmem_limit_bytes=VMEM_LIMIT_BYTES),
        name="merge_out_prompt",
    )(h, o, p, p, *weights)


def _merge_out_sample_kernel(h_ref, o_ref, d_ref, *rest, d_ff):
    *param_refs, y_ref = rest
    gd = d_ref.shape[1] // len(POOL_WINDOWS)
    diffs = [d_ref[:, pl.ds(g * gd, gd)] for g in range(len(POOL_WINDOWS))]
    _merge_out_body(h_ref, o_ref, diffs, *param_refs, y_ref, d_ff)


def _merge_out_sample(h, o, d, weights, *, layer):
    whole = lambda a: pl.BlockSpec(a.shape, lambda i: (0, 0))
    return pl.pallas_call(
        functools.partial(_merge_out_sample_kernel, d_ff=weights[-1].shape[1]),
        grid=(1,),
        in_specs=[whole(h), whole(o), whole(d)] + [_layer_spec(a, layer) for a in weights],
        out_specs=whole(h),
        out_shape=jax.ShapeDtypeStruct(h.shape, F32),
        compiler_params=pltpu.CompilerParams(
            dimension_semantics=("arbitrary",), vmem_limit_bytes=VMEM_LIMIT_BYTES),
        name="merge_out_sample",
    )(h, o, d, *weights)


TM_PROMPT = 512
TQ = 512
TK = 128
QC = 256


def kernel(x_prompt, x_sample, cache_k, cache_v, state_pool, page_table, rel_bias, ffn1_norm_pre, ffn1_norm_post, ffn1_w_gate, ffn1_w_up, ffn1_w_down, mix_norm_pre, mix_norm_post, w_in, lambda_q1, lambda_k1, lambda_q2, lambda_k2, attn_subln, pool_w, pool_scale, w_branch_a, w_branch_b, w_out, ffn2_norm_pre, ffn2_norm_post, ffn2_w_gate, ffn2_w_up, ffn2_w_down):
    depth = w_in.shape[0]
    bsz, seq, d_model = x_prompt.shape
    n_seq = x_sample.shape[0]
    n_pages = page_table.shape[1]
    past_len = n_pages * PAGE_SIZE
    n_phys = cache_k.shape[1]
    w = ATTN_W
    qkvp_w = 4 * w

    wg1, wu1 = ffn1_w_gate.astype(BF16), ffn1_w_up.astype(BF16)
    wg2, wu2 = ffn2_w_gate.astype(BF16), ffn2_w_up.astype(BF16)
    wd1 = ffn1_w_down.astype(BF16)
    wd2 = ffn2_w_down.astype(BF16)
    w_qkvp = w_in[:, :, :qkvp_w].astype(BF16)
    w_gate = w_in[:, :, qkvp_w:].astype(BF16)
    wa = w_branch_a.astype(BF16)
    wb = w_branch_b.astype(BF16)
    wo = w_out.astype(BF16)
    pw = pool_w.astype(BF16)
    row = lambda a: a[:, None, :]
    lamv = jnp.stack([lambda_q1, lambda_k1, lambda_q2, lambda_k2], axis=1)
    in_w = (row(ffn1_norm_pre), row(ffn1_norm_post), wg1, wu1, wd1, row(mix_norm_pre), w_qkvp)
    out_w = (row(mix_norm_pre), w_gate, pw, row(pool_scale), wa, wb, wo,
             row(mix_norm_post), row(ffn2_norm_pre), row(ffn2_norm_post), wg2, wu2, wd2)
    subln_col = attn_subln[:, :, None]
    subln_row = attn_subln[:, None, :]

    bias_tab = _bias_by_distance(rel_bias, max(TQ + TK, past_len + 1))
    bias_tiles = _prompt_bias_tiles(bias_tab[:, :TQ + TK], TQ, TK)
    bias_past, bias_self = _sample_bias_tables(bias_tab, n_pages)
    ck = cache_k.reshape(depth, n_phys, PAGE_SIZE * N_HEADS, QK_DIM)
    cv = cache_v.reshape(depth, n_phys, PAGE_SIZE * N_HEADS, V_DIM)

    xp = x_prompt
    xs = x_sample.reshape(n_seq, d_model)
    k_prompt = jnp.zeros((depth, bsz, seq, N_HEADS, QK_DIM), F32)
    v_prompt = jnp.zeros((depth, bsz, seq, N_HEADS, V_DIM), F32)
    outs = [[] for _ in range(4)]
    for l in range(depth):
        lam_init = 0.8 - 0.6 * math.exp(-0.3 * l)
        h, p, qb, kb, vb, k_prompt, v_prompt = _ffn_in_prompt(
            xp, in_w, k_prompt, v_prompt, layer=l, tm=TM_PROMPT)
        o = _attn_prompt(qb, kb, vb, bias_tiles, lamv, subln_col,
                         layer=l, tq=TQ, tk=TK, qc=QC, lam_init=lam_init)
        xp = _merge_out_prompt(h, o, p, out_w, layer=l, tm=TM_PROMPT)
        outs[0].append(p[:, seq - POOL_STATE:, :])

        hs, qs, ks, vs, ps = _ffn_in_sample(xs, in_w, layer=l)
        by_head = lambda a: a.reshape(n_seq, N_HEADS, QK_DIM)
        o_s = _attn_sample(page_table, by_head(qs), by_head(ks), by_head(vs), ck, cv,
                           bias_past, bias_self, lamv, subln_row, layer=l, lam_init=lam_init)
        d_s, new_state = _pool_sample(state_pool[l].reshape(n_seq, POOL_STATE * w), ps,
                                      past_len=past_len)
        xs = _merge_out_sample(hs, o_s.reshape(n_seq, w), d_s, out_w, layer=l)
        outs[1].append(ks.reshape(n_seq, 1, N_HEADS, QK_DIM))
        outs[2].append(vs.reshape(n_seq, 1, N_HEADS, V_DIM))
        outs[3].append(new_state.reshape(n_seq, POOL_STATE, w))

    stacked = [jnp.stack(o) for o in outs]
    return (xp, xs.reshape(n_seq, 1, d_model), k_prompt, v_prompt, stacked[0],
            stacked[1], stacked[2], stacked[3])
```

```python
import functools
import math

import jax
import jax.numpy as jnp
from jax import lax
from jax.experimental import pallas as pl
from jax.experimental.pallas import tpu as pltpu

F32 = jnp.float32
BF16 = jnp.bfloat16

N_HEADS = 4
HEAD_DIM = 64
QK_DIM = 2 * HEAD_DIM
V_DIM = 2 * HEAD_DIM
ATTN_W = N_HEADS * V_DIM
POOL_WINDOWS = (2, 4, 8, 16)
POOL_STATE = max(POOL_WINDOWS) - 1
N_BUCKETS = 32
MAX_DISTANCE = 128
PAGE_SIZE = 128
NORM_EPS = 1e-6
NEG_INF = -1e30
SCALE = HEAD_DIM ** -0.5
LOG2E = math.log2(math.e)

LANES = 128
SUBLANES = 8
VMEM_LIMIT_BYTES = 56 * 1024 * 1024
MXU_TILE = 256
FF_CHUNK = 6 * MXU_TILE
PAGE_GROUP = LANES // (4 * N_HEADS)
PAGE_BUFFERS = 3


def _rms(x, g):
    ms = jnp.mean(x * x, axis=-1, keepdims=True)
    return x * lax.rsqrt(ms + NORM_EPS) * g


def _layer_spec(stacked, layer):
    zeros = (0,) * (stacked.ndim - 1)
    return pl.BlockSpec((None,) + stacked.shape[1:], lambda *_: (layer,) + zeros,
                        pipeline_mode=pl.Buffered(1))


def _swiglu(u, wg_ref, wu_ref, wd_ref, d_ff):
    f = None
    for lo in range(0, d_ff, FF_CHUNK):
        hi = min(lo + FF_CHUNK, d_ff)
        g = jnp.dot(u, wg_ref[:, lo:hi], preferred_element_type=F32)
        up = jnp.dot(u, wu_ref[:, lo:hi], preferred_element_type=F32)
        act = (g * jax.nn.sigmoid(g) * up).astype(BF16)
        part = jnp.dot(act, wd_ref[lo:hi, :], preferred_element_type=F32)
        f = part if f is None else f + part
    return f


def _ffn_in_body(x_ref, pre_ref, post_ref, wg_ref, wu_ref, wd_ref, mpre_ref, win_ref, d_ff):
    x = x_ref[...]
    f = _swiglu(_rms(x, pre_ref[...]).astype(BF16), wg_ref, wu_ref, wd_ref, d_ff)
    h = x + 0.5 * _rms(f, post_ref[...])
    u = _rms(h, mpre_ref[...]).astype(BF16)
    proj = jnp.dot(u, win_ref[...], preferred_element_type=F32)
    w = ATTN_W
    return h, proj[:, :w], proj[:, w:2 * w], proj[:, 2 * w:3 * w], proj[:, 3 * w:]


def _ffn_in_prompt_kernel(x_ref, *rest, d_ff):
    *param_refs, kbuf_ref, vbuf_ref, h_ref, p_ref, qb_ref, kb_ref, vb_ref, k5_ref, v5_ref = rest
    del kbuf_ref, vbuf_ref
    h, q, k, v, p = _ffn_in_body(x_ref, *param_refs, d_ff)
    h_ref[...] = h
    p_ref[...] = p
    qb_ref[...] = (q * (SCALE * LOG2E)).astype(BF16)
    kb_ref[...] = k.astype(BF16)
    vb_ref[...] = v.astype(BF16)
    for hd in range(N_HEADS):
        k5_ref[:, hd, :] = k[:, hd * QK_DIM:(hd + 1) * QK_DIM]
        v5_ref[:, hd, :] = v[:, hd * V_DIM:(hd + 1) * V_DIM]


def _ffn_in_prompt(x, weights, kbuf, vbuf, *, layer, tm):
    b, t, d = x.shape
    d_ff = weights[4].shape[1]
    w = ATTN_W
    tok = lambda width: pl.BlockSpec((None, tm, width), lambda bi, i: (bi, i, 0))
    kv5 = pl.BlockSpec((None, None, tm, N_HEADS, QK_DIM), lambda bi, i: (layer, bi, i, 0, 0))
    any_spec = pl.BlockSpec(memory_space=pl.ANY)
    n_in = 1 + len(weights)
    return pl.pallas_call(
        functools.partial(_ffn_in_prompt_kernel, d_ff=d_ff),
        grid=(b, t // tm),
        in_specs=[tok(d)] + [_layer_spec(a, layer) for a in weights] + [any_spec, any_spec],
        out_specs=[tok(d), tok(w), tok(w), tok(w), tok(w), kv5, kv5],
        out_shape=[jax.ShapeDtypeStruct((b, t, d), F32),
                   jax.ShapeDtypeStruct((b, t, w), F32),
                   jax.ShapeDtypeStruct((b, t, w), BF16),
                   jax.ShapeDtypeStruct((b, t, w), BF16),
                   jax.ShapeDtypeStruct((b, t, w), BF16),
                   jax.ShapeDtypeStruct(kbuf.shape, F32),
                   jax.ShapeDtypeStruct(vbuf.shape, F32)],
        input_output_aliases={n_in: 5, n_in + 1: 6},
        compiler_params=pltpu.CompilerParams(
            dimension_semantics=("parallel", "parallel"), vmem_limit_bytes=VMEM_LIMIT_BYTES),
        name="ffn_in_prompt",
    )(x, *weights, kbuf, vbuf)


def _ffn_in_sample_kernel(x_ref, *rest, d_ff):
    *param_refs, h_ref, q_ref, k_ref, v_ref, p_ref = rest
    h, q, k, v, p = _ffn_in_body(x_ref, *param_refs, d_ff)
    h_ref[...] = h
    q_ref[...] = q * SCALE
    k_ref[...] = k
    v_ref[...] = v
    p_ref[...] = p


def _ffn_in_sample(x, weights, *, layer):
    n, d = x.shape
    d_ff = weights[4].shape[1]
    whole = lambda width: pl.BlockSpec((n, width), lambda i: (0, 0))
    return pl.pallas_call(
        functools.partial(_ffn_in_sample_kernel, d_ff=d_ff),
        grid=(1,),
        in_specs=[whole(d)] + [_layer_spec(a, layer) for a in weights],
        out_specs=[whole(d)] + [whole(ATTN_W)] * 4,
        out_shape=[jax.ShapeDtypeStruct((n, d), F32)] + [jax.ShapeDtypeStruct((n, ATTN_W), F32)] * 4,
        compiler_params=pltpu.CompilerParams(
            dimension_semantics=("arbitrary",), vmem_limit_bytes=VMEM_LIMIT_BYTES),
        name="ffn_in_sample",
    )(x, *weights)


def _lambda_value(lamv, lam_init):
    s1 = jnp.sum(lamv[0:1] * lamv[1:2], axis=1, keepdims=True)
    s2 = jnp.sum(lamv[2:3] * lamv[3:4], axis=1, keepdims=True)
    return jnp.exp(s1) - jnp.exp(s2) + lam_init


def _bias_by_distance(rel_bias, n):
    dist = jnp.arange(n, dtype=jnp.int32)
    max_exact = N_BUCKETS // 2
    nf = jnp.maximum(dist, 1).astype(F32)
    large = max_exact + (jnp.log(nf / max_exact) / math.log(MAX_DISTANCE / max_exact)
                         * (N_BUCKETS - max_exact)).astype(jnp.int32)
    large = jnp.minimum(large, N_BUCKETS - 1)
    bucket = jnp.where(dist < max_exact, dist, large)
    return rel_bias[bucket].T.astype(F32)


def _attn_prompt_kernel(q_ref, k_ref, v_ref, bias_ref, lamv_ref, subln_ref, o_ref,
                        vt_ref, q2_ref, m_ref, l_ref, acc_ref, *, tq, tk, qc, lam_init):
    qi = pl.program_id(2)
    ratio = tq // tk
    n_special = bias_ref.shape[0]
    n_chunks = 2 * tq // qc

    @pl.when(qi == 0)
    def _():
        def transpose_tile(j, carry):
            start = pl.multiple_of(j * tk, tk)
            vt_ref[j] = v_ref[pl.ds(start, tk), :].T
            return carry
        lax.fori_loop(0, vt_ref.shape[0], transpose_tile, 0)

    qt = q_ref[...].T
    row = lax.broadcasted_iota(jnp.int32, qt.shape, 0)
    zero = jnp.zeros_like(qt)
    q2_ref[:, pl.ds(0, tq)] = jnp.where(row < HEAD_DIM, qt, zero)
    q2_ref[:, pl.ds(tq, tq)] = jnp.where(row >= HEAD_DIM, qt, zero)
    m_ref[...] = jnp.full(m_ref.shape, NEG_INF, F32)
    l_ref[...] = jnp.zeros(l_ref.shape, F32)
    acc_ref[...] = jnp.zeros(acc_ref.shape, F32)

    def step(kj, special):
        start = pl.multiple_of(kj * tk, tk)
        k = k_ref[pl.ds(start, tk), :]
        vt = vt_ref[kj]
        for c in range(n_chunks):
            q_lo = (c * qc) % tq
            if special is not None and q_lo + qc - 1 < (special - 1) * tk:
                continue
            cols = pl.ds(c * qc, qc)
            s = jnp.dot(k, q2_ref[:, cols], preferred_element_type=F32)
            if special is not None:
                s = s + bias_ref[special, :, pl.ds(q_lo, qc)]
            m_old = m_ref[:, cols]
            m_new = jnp.maximum(m_old, jnp.max(s, axis=0, keepdims=True))
            alpha = jnp.exp2(m_old - m_new)
            p = jnp.exp2(s - m_new)
            l_ref[:, cols] = alpha * l_ref[:, cols] + jnp.sum(p, axis=0, keepdims=True)
            pv = jnp.dot(vt, p.astype(BF16), preferred_element_type=F32)
            acc_ref[:, cols] = alpha * acc_ref[:, cols] + pv
            m_ref[:, cols] = m_new

    def run_tiles(first, specials):
        for t, special in enumerate(specials):
            step(first + t, special)

    n_far_blocks = jnp.maximum(qi - 1, 0)

    def far_group(g, carry):
        run_tiles(g * 4 * ratio, [None] * (4 * ratio))
        return carry

    lax.fori_loop(0, n_far_blocks // 4, far_group, 0)

    @pl.when(n_far_blocks % 4 >= 2)
    def _():
        run_tiles((n_far_blocks // 4) * 4 * ratio, [None] * (2 * ratio))

    @pl.when(n_far_blocks % 2 == 1)
    def _():
        run_tiles((n_far_blocks - 1) * ratio, [None] * ratio)

    @pl.when(qi > 0)
    def _():
        run_tiles((qi - 1) * ratio, [None] * (ratio - 1) + list(range(n_special)))

    @pl.when(qi == 0)
    def _():
        run_tiles(0, list(range(1, n_special)))

    l = l_ref[...]
    acc = acc_ref[...]
    lam = _lambda_value(lamv_ref[...], lam_init)
    o = acc[:, :tq] / l[:, :tq] - lam * (acc[:, tq:] / l[:, tq:])
    ms = jnp.mean(o * o, axis=0, keepdims=True)
    y = o * lax.rsqrt(ms + NORM_EPS) * subln_ref[...] * (1.0 - lam_init)
    o_ref[...] = y.T.astype(BF16)


def _prompt_bias_tiles(bias_tab, tq, tk):
    n_special = tq // tk + 1
    rows = n_special * tk
    period = rows + tq
    x = jnp.arange(period, dtype=jnp.int32)
    dist = jnp.where(x < tq, x, x - period) + tk
    shifted = (bias_tab - bias_tab[:, -1:]) * LOG2E
    vals = jnp.where(dist < 0, NEG_INF, shifted[:, jnp.clip(dist, 0, bias_tab.shape[1] - 1)])
    flat = jnp.tile(vals, (1, rows))[:, :rows * (period - 1)]
    toeplitz = flat.reshape(-1, rows, period - 1)[:, :, :tq]
    return toeplitz.reshape(-1, n_special, tk, tq).astype(F32)


def _attn_prompt(qb, kb, vb, bias_tiles, lamv, subln_col, *, layer, tq, tk, qc, lam_init):
    b, t, w = qb.shape
    nk = t // tk
    n_special = bias_tiles.shape[1]
    grid = (b, N_HEADS, t // tq)
    return pl.pallas_call(
        functools.partial(_attn_prompt_kernel, tq=tq, tk=tk, qc=qc, lam_init=lam_init),
        grid=grid,
        in_specs=[
            pl.BlockSpec((None, tq, QK_DIM), lambda bi, h, qi: (bi, qi, h)),
            pl.BlockSpec((None, t, QK_DIM), lambda bi, h, qi: (bi, 0, h)),
            pl.BlockSpec((None, t, V_DIM), lambda bi, h, qi: (bi, 0, h)),
            pl.BlockSpec((None, n_special, tk, tq), lambda bi, h, qi: (h, 0, 0, 0)),
            pl.BlockSpec((None,) + lamv.shape[1:], lambda bi, h, qi: (layer, 0, 0)),
            pl.BlockSpec((None,) + subln_col.shape[1:], lambda bi, h, qi: (layer, 0, 0)),
        ],
        out_specs=pl.BlockSpec((None, tq, V_DIM), lambda bi, h, qi: (bi, qi, h)),
        out_shape=jax.ShapeDtypeStruct((b, t, w), BF16),
        scratch_shapes=[pltpu.VMEM((nk, V_DIM, tk), BF16),
                        pltpu.VMEM((QK_DIM, 2 * tq), BF16),
                        pltpu.VMEM((1, 2 * tq), F32),
                        pltpu.VMEM((1, 2 * tq), F32),
                        pltpu.VMEM((V_DIM, 2 * tq), F32)],
        compiler_params=pltpu.CompilerParams(
            dimension_semantics=("parallel", "parallel", "arbitrary"),
            vmem_limit_bytes=VMEM_LIMIT_BYTES),
        name="attn_prompt",
    )(qb, kb, vb, bias_tiles, lamv, subln_col)


def _attn_sample_kernel(pt_ref, q_ref, kn_ref, vn_ref, bias_ref, bias_self_ref, lamv_ref,
                        subln_ref, *rest, n_pages, layer, lam_init):
    ck_ref, cv_ref, o_ref, kbuf_ref, vbuf_ref, sem_ref, w_ref, ws_ref = rest
    step = pl.program_id(0)
    n_seq = pl.num_programs(0) - 1
    qh, kn, vn = q_ref[...], kn_ref[...], vn_ref[...]

    def page_copies(t):
        t = jnp.asarray(t, jnp.int32)
        slot = lax.rem(t, jnp.int32(PAGE_BUFFERS))
        k_seq = jnp.minimum(t, n_seq - 1)
        v_seq = jnp.clip(t - 1, 0, n_seq - 1)
        copies = []
        for j in range(n_pages):
            copies.append(pltpu.make_async_copy(
                ck_ref.at[layer, pt_ref[k_seq * n_pages + j]], kbuf_ref.at[slot, j],
                sem_ref.at[0, slot]))
            copies.append(pltpu.make_async_copy(
                cv_ref.at[layer, pt_ref[v_seq * n_pages + j]], vbuf_ref.at[slot, j],
                sem_ref.at[1, slot]))
        return copies

    @pl.when(step == 0)
    def _():
        w_ref[...] = jnp.zeros(w_ref.shape, F32)
        ws_ref[...] = jnp.zeros(ws_ref.shape, F32)
        for t in range(PAGE_BUFFERS - 1):
            for c in page_copies(t):
                c.start()

    for c in page_copies(step):
        c.wait()
    for c in page_copies(step + PAGE_BUFFERS - 1):
        c.start()
    slot = lax.rem(step, jnp.int32(PAGE_BUFFERS))
    kp_refs = [kbuf_ref.at[slot, j] for j in range(n_pages)]
    vp_refs = [vbuf_ref.at[slot, j] for j in range(n_pages)]

    nt = (((1,), (1,)), ((), ()))
    half = PAGE_SIZE * N_HEADS // 2
    n_col = 2 * N_HEADS
    page_cols = 2 * n_col
    n_groups = n_pages // PAGE_GROUP

    lane = lax.broadcasted_iota(jnp.int32, qh.shape, 1)
    zero = jnp.zeros_like(qh)
    q8 = jnp.concatenate([jnp.where(lane < HEAD_DIM, qh, zero),
                          jnp.where(lane >= HEAD_DIM, qh, zero)], axis=0)
    z8 = jnp.zeros_like(q8)
    rt = jnp.concatenate([jnp.concatenate([q8, z8], axis=1),
                          jnp.concatenate([z8, q8], axis=1)], axis=0)
    wide = (PAGE_GROUP * page_cols, PAGE_GROUP * 2 * LANES)
    own = (lax.broadcasted_iota(jnp.int32, wide, 0) // page_cols
           == lax.broadcasted_iota(jnp.int32, wide, 1) // (2 * LANES))
    rt_cat = jnp.where(own, jnp.tile(rt, (PAGE_GROUP, PAGE_GROUP)), 0.0)

    def paired(ref):
        return jnp.concatenate([ref[pl.ds(0, half), :], ref[pl.ds(half, half), :]], axis=1)

    def spread(x16, fill):
        return jnp.concatenate(
            [x16, jnp.full((1, (PAGE_GROUP - 1) * page_cols), fill, F32)], axis=1)

    def over_columns(x, op):
        shift = n_col
        while shift < x.shape[1]:
            x = op(x, pltpu.roll(x, shift, 1))
            shift *= 2
        return x

    esh = (PAGE_GROUP * page_cols, 2 * LANES)
    erow = lax.broadcasted_iota(jnp.int32, esh, 0)
    first_half_lane = lax.broadcasted_iota(jnp.int32, esh, 1) < LANES
    first_half_col = erow % page_cols < n_col

    s_groups = []
    acc = jnp.zeros((half, V_DIM), F32)
    for grp in range(n_groups):
        pages = kp_refs[grp * PAGE_GROUP:(grp + 1) * PAGE_GROUP]
        lhs = jnp.concatenate([paired(kp) for kp in pages], axis=1)
        s_groups.append(lax.dot_general(lhs, rt_cat, nt, preferred_element_type=F32)
                        + bias_ref[grp])
        w_prev = w_ref[grp]
        for g in range(PAGE_GROUP):
            expand = ((erow // page_cols == g) & (first_half_col == first_half_lane)).astype(F32)
            wb = jnp.dot(w_prev, expand, preferred_element_type=F32)
            vp = vp_refs[grp * PAGE_GROUP + g]
            acc = (acc + wb[:, :LANES] * vp[pl.ds(0, half), :]
                   + wb[:, LANES:] * vp[pl.ds(half, half), :])

    acc8 = jnp.sum(acc.reshape(half // SUBLANES, SUBLANES, V_DIM), axis=0)
    expand_self = (first_half_col == first_half_lane)[:page_cols].astype(F32)
    wb_self = jnp.dot(ws_ref[...], expand_self, preferred_element_type=F32)
    o = acc8[:N_HEADS] + acc8[N_HEADS:] + wb_self[:N_HEADS, :LANES] * vn
    ms = jnp.mean(o * o, axis=1, keepdims=True)
    o_ref[...] = o * lax.rsqrt(ms + NORM_EPS) * subln_ref[...] * (1.0 - lam_init)

    kn8 = jnp.concatenate([kn, kn], axis=0)
    s_self = lax.dot_general(jnp.concatenate([kn8, jnp.zeros_like(kn8)], axis=1), rt, nt,
                             preferred_element_type=F32) + bias_self_ref[...]
    m_in = spread(jnp.max(s_self, axis=0, keepdims=True), NEG_INF)
    for s in s_groups:
        m_in = jnp.maximum(m_in, jnp.max(s, axis=0, keepdims=True))
    m = over_columns(m_in, jnp.maximum)
    p_groups = [jnp.exp(s - m) for s in s_groups]
    p_self = jnp.exp(s_self - m[:, :page_cols])
    l_in = spread(jnp.sum(p_self, axis=0, keepdims=True), 0.0)
    for p in p_groups:
        l_in = l_in + jnp.sum(p, axis=0, keepdims=True)
    l = over_columns(l_in, jnp.add)
    lam = _lambda_value(lamv_ref[...], lam_init)
    col = lax.broadcasted_iota(jnp.int32, l.shape, 1) % n_col
    coef = jnp.where(col < N_HEADS, 1.0, -lam) / l
    for grp in range(n_groups):
        w_ref[grp] = p_groups[grp] * coef
    ws_ref[...] = p_self * coef[:, :page_cols]

    @pl.when(step == n_seq)
    def _():
        for ahead in range(1, PAGE_BUFFERS):
            for c in page_copies(step + ahead):
                c.wait()


def _sample_bias_tables(bias_tab, n_pages):
    past_len = n_pages * PAGE_SIZE
    half_tok = PAGE_SIZE // 2
    rev = bias_tab[:, 1:past_len + 1][:, ::-1]
    vals = rev.reshape(N_HEADS, n_pages, 2, half_tok).transpose(1, 3, 0, 2)
    same_head = jnp.eye(N_HEADS, dtype=bool)
    full = jnp.where(same_head[None, None, :, None, None, :],
                     vals[:, :, :, :, None, None], NEG_INF)
    full = jnp.broadcast_to(full, (n_pages, half_tok, N_HEADS, 2, 2, N_HEADS))
    rows = half_tok * N_HEADS
    bias_past = full.reshape(n_pages // PAGE_GROUP, PAGE_GROUP, rows, 4 * N_HEADS)
    bias_past = bias_past.transpose(0, 2, 1, 3).reshape(n_pages // PAGE_GROUP, rows, LANES)
    self_vals = jnp.where(same_head[:, None, :], bias_tab[:, 0][:, None, None], NEG_INF)
    self_vals = jnp.broadcast_to(self_vals, (N_HEADS, 2, N_HEADS)).reshape(N_HEADS, 2 * N_HEADS)
    top = jnp.concatenate([self_vals, jnp.full_like(self_vals, NEG_INF)], axis=1)
    bias_self = jnp.concatenate([top, jnp.full_like(top, NEG_INF)], axis=0)
    return bias_past.astype(F32), bias_self.astype(F32)


def _attn_sample(page_table, q, k_new, v_new, cache_k, cache_v, bias_past, bias_self, lamv,
                 subln_row, *, layer, lam_init):
    n_seq, n_pages = page_table.shape
    scored = lambda s: jnp.minimum(s, n_seq - 1)
    finished = lambda s: jnp.maximum(s - 1, 0)
    head_spec = lambda seq_of: pl.BlockSpec((None, N_HEADS, QK_DIM),
                                            lambda s, pt: (seq_of(s), 0, 0))
    const2 = lambda a: pl.BlockSpec(a.shape, lambda s, pt: (0,) * a.ndim)
    layered = lambda a: pl.BlockSpec((None,) + a.shape[1:], lambda s, pt: (layer, 0, 0))

    any_spec = pl.BlockSpec(memory_space=pl.ANY)
    page_buf = pltpu.VMEM((PAGE_BUFFERS, n_pages, PAGE_SIZE * N_HEADS, QK_DIM), F32)
    grid_spec = pltpu.PrefetchScalarGridSpec(
        num_scalar_prefetch=1,
        grid=(n_seq + 1,),
        in_specs=[head_spec(scored), head_spec(scored), head_spec(finished), const2(bias_past),
                  const2(bias_self), layered(lamv), layered(subln_row), any_spec, any_spec],
        out_specs=head_spec(finished),
        scratch_shapes=[page_buf, page_buf,
                        pltpu.SemaphoreType.DMA((2, PAGE_BUFFERS)),
                        pltpu.VMEM(bias_past.shape, F32),
                        pltpu.VMEM(bias_self.shape, F32)],
    )
    return pl.pallas_call(
        functools.partial(_attn_sample_kernel, n_pages=n_pages, layer=layer, lam_init=lam_init),
        grid_spec=grid_spec,
        out_shape=jax.ShapeDtypeStruct(q.shape, F32),
        compiler_params=pltpu.CompilerParams(
            dimension_semantics=("arbitrary",), vmem_limit_bytes=VMEM_LIMIT_BYTES),
        name="attn_sample",
    )(page_table.reshape(-1), q, k_new, v_new, bias_past, bias_self, lamv, subln_row,
      cache_k, cache_v)


POOL_HALO = POOL_STATE + 1


def _pool_diff_prompt(p_ref, halo_ref, ext_ref, tile_index):
    tm = p_ref.shape[0]
    p = p_ref[...]
    ext_ref[pl.ds(0, POOL_HALO), :] = jnp.where(tile_index > 0, halo_ref[...], 0.0)
    ext_ref[pl.ds(POOL_HALO, tm), :] = p
    gd = p.shape[1] // len(POOL_WINDOWS)
    pos1 = tile_index * tm + lax.broadcasted_iota(jnp.int32, (tm, gd), 0) + 1
    diffs = []
    for g, w in enumerate(POOL_WINDOWS):
        lanes = pl.ds(g * gd, gd)
        ws = ext_ref[pl.ds(POOL_HALO, tm), lanes]
        for j in range(1, w):
            ws = ws + ext_ref[pl.ds(POOL_HALO - j, tm), lanes]
        cnt = jnp.minimum(pos1, w).astype(F32)
        diffs.append(ws / cnt - p[:, g * gd:(g + 1) * gd])
    return diffs


def _pool_sample_kernel(state_ref, p_ref, d_ref, new_ref, *, past_len):
    w_tot = p_ref.shape[1]
    gd = w_tot // len(POOL_WINDOWS)
    p = p_ref[...]
    for g, w in enumerate(POOL_WINDOWS):
        ws = p[:, g * gd:(g + 1) * gd]
        for j in range(1, w):
            row = POOL_STATE - j
            ws = ws + state_ref[:, pl.ds(row * w_tot + g * gd, gd)]
        cnt = float(min(past_len + 1, w))
        d_ref[:, pl.ds(g * gd, gd)] = ws / cnt - p[:, g * gd:(g + 1) * gd]
    keep = (POOL_STATE - 1) * w_tot
    new_ref[:, pl.ds(0, keep)] = state_ref[:, pl.ds(w_tot, keep)]
    new_ref[:, pl.ds(keep, w_tot)] = p


def _pool_sample(state, p, *, past_len):
    return pl.pallas_call(
        functools.partial(_pool_sample_kernel, past_len=past_len),
        out_shape=[jax.ShapeDtypeStruct(p.shape, F32), jax.ShapeDtypeStruct(state.shape, F32)],
        name="pool_sample",
    )(state, p)


def _merge_out_body(h_ref, o_ref, diffs, mpre_ref, wgate_ref, poolw_ref, pscale_ref, wa_ref,
                    wb_ref, wout_ref, mpost_ref, pre2_ref, post2_ref, wg_ref, wu_ref, wd_ref, y_ref,
                    d_ff):
    h = h_ref[...]
    d_model = h.shape[1]
    u = _rms(h, mpre_ref[...]).astype(BF16)
    gates = jnp.dot(u, wgate_ref[...], preferred_element_type=F32)
    pooled = jnp.concatenate(
        [jnp.dot(d.astype(BF16), poolw_ref[g], preferred_element_type=F32)
         for g, d in enumerate(diffs)], axis=1) * pscale_ref[...]
    branch_a = jnp.dot(o_ref[...].astype(BF16), wa_ref[...], preferred_element_type=F32)
    branch_b = jnp.dot(pooled.astype(BF16), wb_ref[...], preferred_element_type=F32)
    merged = (jax.nn.sigmoid(gates[:, :d_model]) * branch_a
              + jax.nn.sigmoid(gates[:, d_model:]) * branch_b)
    mixed = jnp.dot(merged.astype(BF16), wout_ref[...], preferred_element_type=F32)
    h2 = h + _rms(mixed, mpost_ref[...])
    f = _swiglu(_rms(h2, pre2_ref[...]).astype(BF16), wg_ref, wu_ref, wd_ref, d_ff)
    y_ref[...] = h2 + 0.5 * _rms(f, post2_ref[...])


def _merge_out_prompt_kernel(h_ref, o_ref, p_ref, halo_ref, *rest, d_ff):
    *param_refs, y_ref, ext_ref = rest
    diffs = _pool_diff_prompt(p_ref, halo_ref, ext_ref, pl.program_id(1))
    _merge_out_body(h_ref, o_ref, diffs, *param_refs, y_ref, d_ff)


def _merge_out_prompt(h, o, p, weights, *, layer, tm):
    b, t, dm = h.shape
    w = p.shape[-1]
    tok = lambda width: pl.BlockSpec((None, tm, width), lambda bi, i: (bi, i, 0))
    per = tm // POOL_HALO
    halo = pl.BlockSpec((None, POOL_HALO, w), lambda bi, i: (bi, jnp.maximum(i * per - 1, 0), 0))
    return pl.pallas_call(
        functools.partial(_merge_out_prompt_kernel, d_ff=weights[-1].shape[1]),
        grid=(b, t // tm),
        in_specs=[tok(dm), tok(w), tok(w), halo] + [_layer_spec(a, layer) for a in weights],
        out_specs=tok(dm),
        out_shape=jax.ShapeDtypeStruct(h.shape, F32),
        scratch_shapes=[pltpu.VMEM((tm + POOL_HALO, w), F32)],
        compiler_params=pltpu.CompilerParams(
            dimension_semantics=("parallel", "parallel"), vmem_limit_bytes=VMEM_LIMIT_BYTES),
        name="merge_out_prompt",
    )(h, o, p, p, *weights)


def _merge_out_sample_kernel(h_ref, o_ref, d_ref, *rest, d_ff):
    *param_refs, y_ref = rest
    gd = d_ref.shape[1] // len(POOL_WINDOWS)
    diffs = [d_ref[:, pl.ds(g * gd, gd)] for g in range(len(POOL_WINDOWS))]
    _merge_out_body(h_ref, o_ref, diffs, *param_refs, y_ref, d_ff)


def _merge_out_sample(h, o, d, weights, *, layer):
    whole = lambda a: pl.BlockSpec(a.shape, lambda i: (0, 0))
    return pl.pallas_call(
        functools.partial(_merge_out_sample_kernel, d_ff=weights[-1].shape[1]),
        grid=(1,),
        in_specs=[whole(h), whole(o), whole(d)] + [_layer_spec(a, layer) for a in weights],
        out_specs=whole(h),
        out_shape=jax.ShapeDtypeStruct(h.shape, F32),
        compiler_params=pltpu.CompilerParams(
            dimension_semantics=("arbitrary",), vmem_limit_bytes=VMEM_LIMIT_BYTES),
        name="merge_out_sample",
    )(h, o, d, *weights)


TM_PROMPT = 512
TQ = 512
TK = 128
QC = 256


def kernel(x_prompt, x_sample, cache_k, cache_v, state_pool, page_table, rel_bias, ffn1_norm_pre, ffn1_norm_post, ffn1_w_gate, ffn1_w_up, ffn1_w_down, mix_norm_pre, mix_norm_post, w_in, lambda_q1, lambda_k1, lambda_q2, lambda_k2, attn_subln, pool_w, pool_scale, w_branch_a, w_branch_b, w_out, ffn2_norm_pre, ffn2_norm_post, ffn2_w_gate, ffn2_w_up, ffn2_w_down):
    depth = w_in.shape[0]
    bsz, seq, d_model = x_prompt.shape
    n_seq = x_sample.shape[0]
    n_pages = page_table.shape[1]
    past_len = n_pages * PAGE_SIZE
    n_phys = cache_k.shape[1]
    w = ATTN_W
    qkvp_w = 4 * w

    wg1, wu1 = ffn1_w_gate.astype(BF16), ffn1_w_up.astype(BF16)
    wg2, wu2 = ffn2_w_gate.astype(BF16), ffn2_w_up.astype(BF16)
    wd1 = ffn1_w_down.astype(BF16)
    wd2 = ffn2_w_down.astype(BF16)
    w_qkvp = w_in[:, :, :qkvp_w].astype(BF16)
    w_gate = w_in[:, :, qkvp_w:].astype(BF16)
    wa = w_branch_a.astype(BF16)
    wb = w_branch_b.astype(BF16)
    wo = w_out.astype(BF16)
    pw = pool_w.astype(BF16)
    row = lambda a: a[:, None, :]
    lamv = jnp.stack([lambda_q1, lambda_k1, lambda_q2, lambda_k2], axis=1)
    in_w = (row(ffn1_norm_pre), row(ffn1_norm_post), wg1, wu1, wd1, row(mix_norm_pre), w_qkvp)
    out_w = (row(mix_norm_pre), w_gate, pw, row(pool_scale), wa, wb, wo,
             row(mix_norm_post), row(ffn2_norm_pre), row(ffn2_norm_post), wg2, wu2, wd2)
    subln_col = attn_subln[:, :, None]
    subln_row = attn_subln[:, None, :]

    bias_tab = _bias_by_distance(rel_bias, max(TQ + TK, past_len + 1))
    bias_tiles = _prompt_bias_tiles(bias_tab[:, :TQ + TK], TQ, TK)
    bias_past, bias_self = _sample_bias_tables(bias_tab, n_pages)
    ck = cache_k.reshape(depth, n_phys, PAGE_SIZE * N_HEADS, QK_DIM)
    cv = cache_v.reshape(depth, n_phys, PAGE_SIZE * N_HEADS, V_DIM)

    xp = x_prompt
    xs = x_sample.reshape(n_seq, d_model)
    k_prompt = jnp.zeros((depth, bsz, seq, N_HEADS, QK_DIM), F32)
    v_prompt = jnp.zeros((depth, bsz, seq, N_HEADS, V_DIM), F32)
    outs = [[] for _ in range(4)]
    for l in range(depth):
        lam_init = 0.8 - 0.6 * math.exp(-0.3 * l)
        h, p, qb, kb, vb, k_prompt, v_prompt = _ffn_in_prompt(
            xp, in_w, k_prompt, v_prompt, layer=l, tm=TM_PROMPT)
        o = _attn_prompt(qb, kb, vb, bias_tiles, lamv, subln_col,
                         layer=l, tq=TQ, tk=TK, qc=QC, lam_init=lam_init)
        xp = _merge_out_prompt(h, o, p, out_w, layer=l, tm=TM_PROMPT)
        outs[0].append(p[:, seq - POOL_STATE:, :])

        hs, qs, ks, vs, ps = _ffn_in_sample(xs, in_w, layer=l)
        by_head = lambda a: a.reshape(n_seq, N_HEADS, QK_DIM)
        o_s = _attn_sample(page_table, by_head(qs), by_head(ks), by_head(vs), ck, cv,
                           bias_past, bias_self, lamv, subln_row, layer=l, lam_init=lam_init)
        d_s, new_state = _pool_sample(state_pool[l].reshape(n_seq, POOL_STATE * w), ps,
                                      past_len=past_len)
        xs = _merge_out_sample(hs, o_s.reshape(n_seq, w), d_s, out_w, layer=l)
        outs[1].append(ks.reshape(n_seq, 1, N_HEADS, QK_DIM))
        outs[2].append(vs.reshape(n_seq, 1, N_HEADS, V_DIM))
        outs[3].append(new_state.reshape(n_seq, POOL_STATE, w))

    stacked = [jnp.stack(o) for o in outs]
    return (xp, xs.reshape(n_seq, 1, d_model), k_prompt, v_prompt, stacked[0],
            stacked[1], stacked[2], stacked[3])
```
